```python
import jax
import jax.numpy as jnp
from jax import lax
import numpy as np

D_MODEL = 1024
BATCH = 16
SEQ = 2048
DEPTH = 2

CHUNK = 64
EPS = 1e-6
FFN_DIM = 2816
GLA_HEADS = 4
GLA_DK = D_MODEL // 2
GLA_DV = D_MODEL
GLA_HK = GLA_DK // GLA_HEADS
GLA_HV = GLA_DV // GLA_HEADS
GLA_GK_RANK = 16
GLA_GATE_NORM = 16.0
SSM_D_INNER = 2 * D_MODEL
SSM_HEADDIM = 64
SSM_HEADS = SSM_D_INNER // SSM_HEADDIM
SSM_GROUPS = 8
SSM_HPG = SSM_HEADS // SSM_GROUPS
SSM_DSTATE = 128
SSM_CONV = 4
SSM_CONV_DIM = SSM_D_INNER + 2 * SSM_GROUPS * SSM_DSTATE
ATT_HEADS = 16
ATT_HD = D_MODEL // ATT_HEADS
ATT_W = ATT_HEADS * ATT_HD
ATT_LEFT_CHUNKS = 8
ATT_MAX_REL = 256
N_BRANCH = 3
IN_SIZES = (GLA_DK, GLA_DK, GLA_DV, GLA_DV, GLA_GK_RANK, SSM_D_INNER, SSM_CONV_DIM, SSM_HEADS, ATT_W, ATT_W, ATT_W, N_BRANCH * D_MODEL)
N_IN = sum(IN_SIZES)

kernel_name = 'hybrid_gla_ssd_chunkattn_macaron'

F32 = jnp.float32


def _rmsnorm(x, w):
    xf = x.astype(F32)
    y = xf * lax.rsqrt(jnp.mean(xf * xf, axis=-1, keepdims=True) + EPS)
    return (y * w.astype(F32)).astype(x.dtype)


def _swiglu(x, w_gate, w_up, w_down):
    return (jax.nn.silu(x @ w_gate) * (x @ w_up)) @ w_down


def _in_cols(u, w_in, i):
    start = sum(IN_SIZES[:i])
    return u @ w_in[:, start:start + IN_SIZES[i]]


def _to_chunks(t, n_chunks):
    t = t.reshape((t.shape[0], n_chunks, CHUNK) + t.shape[2:])
    return jnp.moveaxis(t, 1, 0)


def _from_chunks(t):
    t = jnp.moveaxis(t, 0, 1)
    return t.reshape((t.shape[0], t.shape[1] * t.shape[2]) + t.shape[3:])


def _gla(q, k, v, gk):
    bsz, seq = q.shape[:2]
    n_chunks = seq // CHUNK
    xs = tuple(_to_chunks(t, n_chunks) for t in (q, k, v, gk))
    mask = jnp.tril(jnp.ones((CHUNK, CHUNK), bool))

    def step(state, inp):
        qc, kc, vc, gc = inp
        b = jnp.cumsum(gc, axis=1)
        b_last = b[:, -1:]
        b_mid = b[:, CHUNK // 2:CHUNK // 2 + 1]
        a = jnp.einsum('bthd,bshd->bhts', qc * jnp.exp(b - b_mid), kc * jnp.exp(b_mid - b))
        a = jnp.where(mask, a, 0.0)
        o = jnp.einsum('bhts,bshv->bthv', a, vc) + jnp.einsum('bthd,bhdv->bthv', qc * jnp.exp(b), state)
        state = jnp.exp(b_last[:, 0])[..., None] * state + jnp.einsum('bshd,bshv->bhdv', kc * jnp.exp(b_last - b), vc)
        return state, o

    s0 = jnp.zeros((bsz, GLA_HEADS, GLA_HK, GLA_HV), F32)
    _, o = lax.scan(step, s0, xs)
    return _from_chunks(o)


def _ssd(xh, bm, cm, dt, a):
    bsz, seq = xh.shape[:2]
    n_chunks = seq // CHUNK
    xs = tuple(_to_chunks(t, n_chunks) for t in (xh, bm, cm, dt, dt * a))
    mask = jnp.tril(jnp.ones((CHUNK, CHUNK), bool))[None, :, :, None, None]

    def step(state, inp):
        xc, bc, cc, dtc, dac = inp
        cum = jnp.cumsum(dac, axis=1)
        seg = cum[:, :, None] - cum[:, None, :]
        decay = jnp.exp(jnp.where(mask, seg, -jnp.inf))
        cb = jnp.einsum('btgn,bsgn->btsg', cc, bc)
        xdt = xc * dtc[..., None]
        y = jnp.einsum('btsg,btsgr,bsgrp->btgrp', cb, decay, xdt)
        y = y + jnp.einsum('btgn,bgrpn->btgrp', cc, state) * jnp.exp(cum)[..., None]
        to_end = jnp.exp(cum[:, -1:] - cum)
        state = jnp.exp(cum[:, -1])[..., None, None] * state + jnp.einsum('bsgn,bsgrp->bgrpn', bc, xdt * to_end[..., None])
        return state, y

    s0 = jnp.zeros((bsz, SSM_GROUPS, SSM_HPG, SSM_HEADDIM, SSM_DSTATE), F32)
    _, y = lax.scan(step, s0, xs)
    return _from_chunks(y)


def _causal_conv(x, w, b):
    out = lax.conv_general_dilated(x, w.astype(x.dtype)[:, None, :], window_strides=(1,),
                                   padding=[(SSM_CONV - 1, 0)],
                                   dimension_numbers=('NWC', 'WIO', 'NWC'),
                                   feature_group_count=x.shape[-1])
    return out + b.astype(x.dtype)


def _chunk_attention(q, k, v, rel_table):
    seq = q.shape[1]
    n_chunks = seq // CHUNK
    pad = ATT_LEFT_CHUNKS * CHUNK
    band = (ATT_LEFT_CHUNKS + 1) * CHUNK
    kp = jnp.pad(k, ((0, 0), (pad, 0), (0, 0), (0, 0)))
    vp = jnp.pad(v, ((0, 0), (pad, 0), (0, 0), (0, 0)))
    qc = _to_chunks(q, n_chunks)
    rel = jnp.arange(CHUNK)[:, None] + pad - jnp.arange(band)[None, :]
    idx = jnp.clip(rel, -ATT_MAX_REL, ATT_MAX_REL) + ATT_MAX_REL
    bias = jnp.transpose(rel_table[idx], (2, 0, 1)).astype(F32)
    scale = ATT_HD ** -0.5

    def one_chunk(inp):
        c, qb = inp
        start = c * CHUNK
        kb = lax.dynamic_slice_in_dim(kp, start, band, axis=1)
        vb = lax.dynamic_slice_in_dim(vp, start, band, axis=1)
        valid = (start - pad + jnp.arange(band)) >= 0
        s = jnp.einsum('bthd,bshd->bhts', qb, kb).astype(F32) * scale + bias
        s = jnp.where(valid, s, -jnp.inf)
        p = jax.nn.softmax(s, axis=-1).astype(vb.dtype)
        return jnp.einsum('bhts,bshd->bthd', p, vb)

    o = lax.map(one_chunk, (jnp.arange(n_chunks), qc))
    return _from_chunks(o)


def _mixer(u, w_in, gla_w_gk, gla_b_gk, gla_norm, ssm_conv_w, ssm_conv_b, ssm_dt_bias, ssm_A_log, ssm_D,
           ssm_norm, attn_rel_bias, gate_bias, w_branch_gla, w_branch_ssm, w_branch_attn, w_out):
    bsz, seq, _ = u.shape
    q = _in_cols(u, w_in, 0).reshape(bsz, seq, GLA_HEADS, GLA_HK).astype(F32) * GLA_HK ** -0.5
    k = _in_cols(u, w_in, 1).reshape(bsz, seq, GLA_HEADS, GLA_HK).astype(F32)
    v = _in_cols(u, w_in, 2).reshape(bsz, seq, GLA_HEADS, GLA_HV).astype(F32)
    r = _in_cols(u, w_in, 3).reshape(bsz, seq, GLA_HEADS, GLA_HV)
    gk = jax.nn.log_sigmoid((_in_cols(u, w_in, 4) @ gla_w_gk + gla_b_gk).astype(F32)) / GLA_GATE_NORM
    gk = gk.reshape(bsz, seq, GLA_HEADS, GLA_HK)
    o_gla = _gla(q, k, v, gk)
    o_gla = (_rmsnorm(o_gla, gla_norm) * jax.nn.silu(r.astype(F32))).reshape(bsz, seq, GLA_DV).astype(u.dtype)
    y_gla = o_gla @ w_branch_gla
    z = _in_cols(u, w_in, 5)
    xbc = jax.nn.silu(_causal_conv(_in_cols(u, w_in, 6), ssm_conv_w, ssm_conv_b))
    xs_, bm, cm = jnp.split(xbc, [SSM_D_INNER, SSM_D_INNER + SSM_GROUPS * SSM_DSTATE], axis=-1)
    dt = jax.nn.softplus(_in_cols(u, w_in, 7).astype(F32) + ssm_dt_bias.astype(F32))
    dt = dt.reshape(bsz, seq, SSM_GROUPS, SSM_HPG)
    a = -jnp.exp(ssm_A_log.astype(F32)).reshape(SSM_GROUPS, SSM_HPG)
    xh = xs_.reshape(bsz, seq, SSM_GROUPS, SSM_HPG, SSM_HEADDIM).astype(F32)
    bm = bm.reshape(bsz, seq, SSM_GROUPS, SSM_DSTATE).astype(F32)
    cm = cm.reshape(bsz, seq, SSM_GROUPS, SSM_DSTATE).astype(F32)
    y = _ssd(xh, bm, cm, dt, a) + ssm_D.astype(F32).reshape(SSM_GROUPS, SSM_HPG)[..., None] * xh
    y = y.reshape(bsz, seq, SSM_D_INNER) * jax.nn.silu(z.astype(F32))
    y = _rmsnorm(y.reshape(bsz, seq, SSM_GROUPS, SSM_D_INNER // SSM_GROUPS),
                 ssm_norm.reshape(SSM_GROUPS, SSM_D_INNER // SSM_GROUPS))
    y_ssm = y.reshape(bsz, seq, SSM_D_INNER).astype(u.dtype) @ w_branch_ssm
    aq = _in_cols(u, w_in, 8).reshape(bsz, seq, ATT_HEADS, ATT_HD)
    ak = _in_cols(u, w_in, 9).reshape(bsz, seq, ATT_HEADS, ATT_HD)
    av = _in_cols(u, w_in, 10).reshape(bsz, seq, ATT_HEADS, ATT_HD)
    o_att = _chunk_attention(aq, ak, av, attn_rel_bias).reshape(bsz, seq, ATT_W)
    y_att = o_att @ w_branch_attn
    g = jax.nn.sigmoid(_in_cols(u, w_in, 11) + gate_bias).reshape(bsz, seq, N_BRANCH, D_MODEL)
    merged = g[:, :, 0] * y_gla + g[:, :, 1] * y_ssm + g[:, :, 2] * y_att
    return merged.astype(u.dtype) @ w_out


def setup_inputs(seed: int = 0) -> dict:
    key = jax.random.key(seed)
    ks = jax.random.split(key, 32)

    def nrm(k, shape, scale):
        return jax.random.normal(k, shape, F32) * scale

    def gain(k, shape):
        return 1.0 + 0.05 * jax.random.normal(k, shape, F32)

    dt = jnp.exp(jax.random.uniform(ks[10], (DEPTH, SSM_HEADS), F32) * (np.log(0.1) - np.log(0.001)) + np.log(0.001))
    return {
        'x': jax.random.normal(ks[0], (BATCH, SEQ, D_MODEL), F32),
        'ffn1_norm': gain(ks[1], (DEPTH, D_MODEL)),
        'ffn1_w_gate': nrm(ks[2], (DEPTH, D_MODEL, FFN_DIM), D_MODEL ** -0.5),
        'ffn1_w_up': nrm(ks[3], (DEPTH, D_MODEL, FFN_DIM), D_MODEL ** -0.5),
        'ffn1_w_down': nrm(ks[4], (DEPTH, FFN_DIM, D_MODEL), FFN_DIM ** -0.5),
        'mix_norm': gain(ks[5], (DEPTH, D_MODEL)),
        'w_in': nrm(ks[6], (DEPTH, D_MODEL, N_IN), D_MODEL ** -0.5),
        'gla_w_gk': nrm(ks[7], (DEPTH, GLA_GK_RANK, GLA_DK), GLA_GK_RANK ** -0.5),
        'gla_b_gk': nrm(ks[8], (DEPTH, GLA_DK), 0.1),
        'gla_norm': gain(ks[9], (DEPTH, GLA_HV)),
        'ssm_conv_w': nrm(ks[11], (DEPTH, SSM_CONV, SSM_CONV_DIM), SSM_CONV ** -0.5),
        'ssm_conv_b': nrm(ks[12], (DEPTH, SSM_CONV_DIM), 0.02),
        'ssm_dt_bias': dt + jnp.log(-jnp.expm1(-dt)),
        'ssm_A_log': jnp.log(jax.random.uniform(ks[13], (DEPTH, SSM_HEADS), F32, 1.0, 16.0)),
        'ssm_D': gain(ks[14], (DEPTH, SSM_HEADS)),
        'ssm_norm': gain(ks[15], (DEPTH, SSM_D_INNER)),
        'attn_rel_bias': nrm(ks[16], (DEPTH, 2 * ATT_MAX_REL + 1, ATT_HEADS), 0.2),
        'gate_bias': nrm(ks[17], (DEPTH, N_BRANCH * D_MODEL), 0.02),
        'w_branch_gla': nrm(ks[18], (DEPTH, GLA_DV, D_MODEL), GLA_DV ** -0.5),
        'w_branch_ssm': nrm(ks[19], (DEPTH, SSM_D_INNER, D_MODEL), SSM_D_INNER ** -0.5),
        'w_branch_attn': nrm(ks[20], (DEPTH, ATT_W, D_MODEL), ATT_W ** -0.5),
        'w_out': nrm(ks[21], (DEPTH, D_MODEL, D_MODEL), D_MODEL ** -0.5),
        'ffn2_norm': gain(ks[22], (DEPTH, D_MODEL)),
        'ffn2_w_gate': nrm(ks[23], (DEPTH, D_MODEL, FFN_DIM), D_MODEL ** -0.5),
        'ffn2_w_up': nrm(ks[24], (DEPTH, D_MODEL, FFN_DIM), D_MODEL ** -0.5),
        'ffn2_w_down': nrm(ks[25], (DEPTH, FFN_DIM, D_MODEL), FFN_DIM ** -0.5),
        'final_norm': gain(ks[26], (D_MODEL,)),
    }


def reference(x, ffn1_norm, ffn1_w_gate, ffn1_w_up, ffn1_w_down, mix_norm, w_in, gla_w_gk, gla_b_gk, gla_norm,
              ssm_conv_w, ssm_conv_b, ssm_dt_bias, ssm_A_log, ssm_D, ssm_norm, attn_rel_bias, gate_bias,
              w_branch_gla, w_branch_ssm, w_branch_attn, w_out, ffn2_norm, ffn2_w_gate, ffn2_w_up, ffn2_w_down,
              final_norm):
    h = x
    for l in range(DEPTH):
        h = h + 0.5 * _swiglu(_rmsnorm(h, ffn1_norm[l]), ffn1_w_gate[l], ffn1_w_up[l], ffn1_w_down[l])
        h = h + _mixer(_rmsnorm(h, mix_norm[l]), w_in[l], gla_w_gk[l], gla_b_gk[l], gla_norm[l],
                       ssm_conv_w[l], ssm_conv_b[l], ssm_dt_bias[l], ssm_A_log[l], ssm_D[l], ssm_norm[l],
                       attn_rel_bias[l], gate_bias[l], w_branch_gla[l], w_branch_ssm[l], w_branch_attn[l], w_out[l])
        h = h + 0.5 * _swiglu(_rmsnorm(h, ffn2_norm[l]), ffn2_w_gate[l], ffn2_w_up[l], ffn2_w_down[l])
    return _rmsnorm(h, final_norm)
```

```python
import functools

import jax
import jax.numpy as jnp
from jax import lax
from jax.experimental import pallas as pl
from jax.experimental.pallas import tpu as pltpu

F32 = jnp.float32
BF16 = jnp.bfloat16

EPS = 1e-6
CHUNK = 64
GLA_HEADS = 4
GLA_GATE_NORM = 16.0
SSM_HEADDIM = 64
SSM_GROUPS = 8
SSM_DSTATE = 128
SSM_CONV = 4
ATT_HD = 64
ATT_LEFT_CHUNKS = 8
ATT_MAX_REL = 256

V7X_VMEM_BYTES = 64 * 1024 * 1024
VMEM_LIMIT_BYTES = V7X_VMEM_BYTES - 8 * 1024 * 1024

TOKEN_BLOCK = 512
SEQ_BLOCK = 256


def _params(*semantics):
    return pltpu.CompilerParams(dimension_semantics=semantics, vmem_limit_bytes=VMEM_LIMIT_BYTES)


def _resident(shape):
    zeros = (0,) * len(shape)
    return pl.BlockSpec(shape, lambda *_: zeros, pipeline_mode=pl.Buffered(1))


def _dot(a, b):
    return jnp.dot(a, b, preferred_element_type=F32)


def _dot_nt(a, b):
    return lax.dot_general(a, b, (((1,), (1,)), ((), ())), preferred_element_type=F32)


def _dot_tn(a, b):
    return lax.dot_general(a, b, (((0,), (0,)), ((), ())), preferred_element_type=F32)


def _rms(x, w):
    return x * lax.rsqrt(jnp.mean(x * x, axis=-1, keepdims=True) + EPS) * w


def _split3(x):
    hi = x.astype(BF16)
    r = x - hi.astype(F32)
    mid = r.astype(BF16)
    lo = (r - mid.astype(F32)).astype(BF16)
    return hi, mid, lo


def _sel_left(sel, x):
    hi, mid, lo = _split3(x)
    return _dot(sel, hi) + _dot(sel, mid) + _dot(sel, lo)


def _sel_right(x, sel):
    hi, mid, lo = _split3(x)
    return _dot(hi, sel) + _dot(mid, sel) + _dot(lo, sel)


def _tri(n, lower):
    r = lax.broadcasted_iota(jnp.int32, (n, n), 0)
    c = lax.broadcasted_iota(jnp.int32, (n, n), 1)
    return (r >= c) if lower else (r <= c)


def _ffn_kernel(h_ref, nw_ref, wg_ref, wu_ref, wd_ref, fw_ref, o_ref, *, f_chunks, final_norm):
    x = h_ref[...]
    xn = _rms(x, nw_ref[...]).astype(BF16)
    f_dim = wg_ref.shape[1]
    fc = f_dim // f_chunks
    acc = None
    for i in range(f_chunks):
        g = _dot(xn, wg_ref[:, i * fc:(i + 1) * fc])
        u = _dot(xn, wu_ref[:, i * fc:(i + 1) * fc])
        a = (g * jax.nn.sigmoid(g) * u).astype(BF16)
        y = _dot(a, wd_ref[i * fc:(i + 1) * fc, :])
        acc = y if acc is None else acc + y
    out = x + 0.5 * acc
    if final_norm:
        out = _rms(out, fw_ref[...])
    o_ref[...] = out


def _ffn(h, norm_w, wg, wu, wd, final_w, final_norm):
    t, d = h.shape
    f_dim = wg.shape[1]
    bm = min(TOKEN_BLOCK, t)
    row = pl.BlockSpec((bm, d), lambda i: (i, 0))
    return pl.pallas_call(
        functools.partial(_ffn_kernel, f_chunks=2, final_norm=final_norm),
        grid=(t // bm,),
        in_specs=[row, _resident((1, d)), _resident((d, f_dim)), _resident((d, f_dim)),
                  _resident((f_dim, d)), _resident((1, d))],
        out_specs=row,
        out_shape=jax.ShapeDtypeStruct((t, d), F32),
        compiler_params=_params("parallel"),
        name="ffn",
    )(h, norm_w, wg, wu, wd, final_w)


def _proj_gla_kernel(h_ref, nw_ref, w_ref, wc_ref, wgk_ref, bgk_ref, o_ref, gk_ref):
    xn = _rms(h_ref[...], nw_ref[...]).astype(BF16)
    o_ref[...] = _dot(xn, w_ref[...]).astype(BF16)
    code = _dot(xn, wc_ref[...]).astype(BF16)
    pre = _dot(code, wgk_ref[...]) + bgk_ref[...]
    gk_ref[...] = jax.nn.log_sigmoid(pre) * (1.0 / GLA_GATE_NORM)


def _proj_ssm_kernel(h_ref, nw_ref, w_ref, wdt_ref, wdtt_ref, dtb_ref, dtbt_ref, o_ref, dt_ref, dtt_ref):
    xn = _rms(h_ref[...], nw_ref[...]).astype(BF16)
    o_ref[...] = _dot(xn, w_ref[...]).astype(BF16)
    dt_ref[...] = jax.nn.softplus(_dot(xn, wdt_ref[...]) + dtb_ref[...])
    dtt_ref[...] = jax.nn.softplus(_dot_nt(wdtt_ref[...], xn) + dtbt_ref[...])


def _proj_att_kernel(h_ref, nw_ref, w_ref, o_ref):
    xn = _rms(h_ref[...], nw_ref[...]).astype(BF16)
    o_ref[...] = _dot(xn, w_ref[...]).astype(BF16)


def _proj_gate_kernel(h_ref, nw_ref, w_ref, b_ref, o_ref):
    xn = _rms(h_ref[...], nw_ref[...]).astype(BF16)
    o_ref[...] = jax.nn.sigmoid(_dot(xn, w_ref[...]) + b_ref[...]).astype(BF16)


def _proj_call(kernel, name, h, consts, outs):
    t, d = h.shape
    bm = min(TOKEN_BLOCK, t)
    in_specs = [pl.BlockSpec((bm, d), lambda i: (i, 0))] + [_resident(c.shape) for c in consts]
    out_specs, out_shape = [], []
    for cols, dtype, transposed in outs:
        if transposed:
            out_specs.append(pl.BlockSpec((cols, bm), lambda i: (0, i)))
            out_shape.append(jax.ShapeDtypeStruct((cols, t), dtype))
        else:
            out_specs.append(pl.BlockSpec((bm, cols), lambda i: (i, 0)))
            out_shape.append(jax.ShapeDtypeStruct((t, cols), dtype))
    return pl.pallas_call(
        kernel, grid=(t // bm,), in_specs=in_specs, out_specs=out_specs, out_shape=out_shape,
        compiler_params=_params("parallel"), name=name,
    )(h, *consts)


def _gla_kernel(x_ref, gk_ref, nw_ref, o_ref, state_ref, *, dk, dv):
    hk, hv = dk // GLA_HEADS, dv // GLA_HEADS
    scale = hk ** -0.5

    @pl.when(pl.program_id(1) == 0)
    def _():
        state_ref[...] = jnp.zeros_like(state_ref)

    tril = _tri(CHUNK, True)
    tril_b = tril.astype(BF16)
    nw = nw_ref[...]
    for j in range(x_ref.shape[0] // CHUNK):
        rows = slice(j * CHUNK, (j + 1) * CHUNK)
        b_all = _sel_left(tril_b, gk_ref[rows, :])
        for h in range(GLA_HEADS):
            q = x_ref[rows, h * hk:(h + 1) * hk].astype(F32) * scale
            k = x_ref[rows, dk + h * hk:dk + (h + 1) * hk].astype(F32)
            v = x_ref[rows, 2 * dk + h * hv:2 * dk + (h + 1) * hv]
            r = x_ref[rows, 2 * dk + dv + h * hv:2 * dk + dv + (h + 1) * hv].astype(F32)
            b = b_all[:, h * hk:(h + 1) * hk]
            b_last = b[CHUNK - 1:CHUNK, :]
            b_mid = b[CHUNK // 2:CHUNK // 2 + 1, :]
            a = _dot_nt((q * jnp.exp(b - b_mid)).astype(BF16), (k * jnp.exp(b_mid - b)).astype(BF16))
            a = jnp.where(tril, a, 0.0)
            st = state_ref[h]
            o = _dot(a.astype(BF16), v) + _dot_nt((q * jnp.exp(b)).astype(BF16), st.astype(BF16))
            state_ref[h] = jnp.exp(b_last) * st + _dot_tn(v, (k * jnp.exp(b_last - b)).astype(BF16))
            o = _rms(o, nw) * (r * jax.nn.sigmoid(r))
            o_ref[rows, h * hv:(h + 1) * hv] = o.astype(BF16)


def _gla(x, gk, norm_w, dk, dv):
    bsz, seq, width = x.shape
    lb = min(SEQ_BLOCK, seq)
    hk, hv = dk // GLA_HEADS, dv // GLA_HEADS
    return pl.pallas_call(
        functools.partial(_gla_kernel, dk=dk, dv=dv),
        grid=(bsz, seq // lb),
        in_specs=[pl.BlockSpec((None, lb, width), lambda b, i: (b, i, 0)),
                  pl.BlockSpec((None, lb, dk), lambda b, i: (b, i, 0)),
                  _resident((1, hv))],
        out_specs=pl.BlockSpec((None, lb, dv), lambda b, i: (b, i, 0)),
        out_shape=jax.ShapeDtypeStruct((bsz, seq, dv), BF16),
        scratch_shapes=[pltpu.VMEM((GLA_HEADS, hv, hk), F32)],
        compiler_params=_params("parallel", "arbitrary"),
        name="gla",
    )(x, gk, norm_w)


def _ssd_kernel(x_ref, dt_ref, dtt_ref, cw_ref, cb_ref, alog_ref, alogt_ref, dexp_ref, nw_ref,
                o_ref, state_ref, ext_ref, y_ref, *, d_inner):
    lb = x_ref.shape[0]
    gn = SSM_GROUPS * SSM_DSTATE
    gw = d_inner // SSM_GROUPS
    hpg = gw // SSM_HEADDIM
    halo = 8

    @pl.when(pl.program_id(1) == 0)
    def _():
        state_ref[...] = jnp.zeros_like(state_ref)
        ext_ref[0:halo, :] = jnp.zeros((halo, ext_ref.shape[1]), F32)

    ext_ref[halo:halo + lb, :] = x_ref[:, d_inner:].astype(F32)
    conv = cb_ref[...] + cw_ref[SSM_CONV - 1:SSM_CONV, :] * ext_ref[halo:halo + lb, :]
    for tap in range(SSM_CONV - 1):
        off = halo - (SSM_CONV - 1) + tap
        conv = conv + cw_ref[tap:tap + 1, :] * ext_ref[off:off + lb, :]
    ext_ref[0:halo, :] = ext_ref[lb:lb + halo, :]
    xbc = conv * jax.nn.sigmoid(conv)

    a_row = -jnp.exp(alog_ref[...])
    a_col = -jnp.exp(alogt_ref[...])
    tril = _tri(CHUNK, True)
    tril_b = tril.astype(BF16)
    triu_b = _tri(CHUNK, False).astype(BF16)
    for j in range(lb // CHUNK):
        rows = slice(j * CHUNK, (j + 1) * CHUNK)
        dt = dt_ref[rows, :]
        dtt = dtt_ref[:, rows]
        cum = _sel_left(tril_b, dt * a_row)
        cumt = _sel_right(dtt * a_col, triu_b)
        for g in range(SSM_GROUPS):
            bg = xbc[rows, d_inner + g * SSM_DSTATE:d_inner + (g + 1) * SSM_DSTATE].astype(BF16)
            cg = xbc[rows, d_inner + gn + g * SSM_DSTATE:d_inner + gn + (g + 1) * SSM_DSTATE].astype(BF16)
            cb = _dot_nt(cg, bg)
            st = state_ref[g]
            y_off = _dot(cg, st.astype(BF16))
            for r in range(hpg):
                h = g * hpg + r
                cols = slice(h * SSM_HEADDIM, (h + 1) * SSM_HEADDIM)
                sub = slice(r * SSM_HEADDIM, (r + 1) * SSM_HEADDIM)
                cum_h = cum[:, h:h + 1]
                last_h = cum[CHUNK - 1:CHUNK, h:h + 1]
                decay = jnp.exp(jnp.where(tril, cum_h - cumt[h:h + 1, :], -jnp.inf))
                xh = xbc[rows, cols]
                xdt = xh * dt[:, h:h + 1]
                y = _dot((cb * decay).astype(BF16), xdt.astype(BF16))
                y = y + y_off[:, sub] * jnp.exp(cum_h)
                y_ref[rows, cols] = y + dexp_ref[:, cols] * xh
                upd = _dot_tn(bg, (xdt * jnp.exp(last_h - cum_h)).astype(BF16))
                state_ref[g, :, sub] = jnp.exp(last_h) * st[:, sub] + upd

    z = x_ref[:, :d_inner].astype(F32)
    y = y_ref[...] * (z * jax.nn.sigmoid(z))
    for g in range(SSM_GROUPS):
        cols = slice(g * gw, (g + 1) * gw)
        o_ref[:, cols] = _rms(y[:, cols], nw_ref[:, cols]).astype(BF16)


def _ssd(x, dt, dtt, conv_w, conv_b, a_log, a_log_t, d_exp, norm_w, d_inner):
    bsz, seq, width = x.shape
    heads = dt.shape[-1]
    lb = min(SEQ_BLOCK, seq)
    nblk = seq // lb
    conv_dim = width - d_inner
    gw = d_inner // SSM_GROUPS
    return pl.pallas_call(
        functools.partial(_ssd_kernel, d_inner=d_inner),
        grid=(bsz, nblk),
        in_specs=[pl.BlockSpec((None, lb, width), lambda b, i: (b, i, 0)),
                  pl.BlockSpec((None, lb, heads), lambda b, i: (b, i, 0)),
                  pl.BlockSpec((heads, lb), lambda b, i: (0, b * nblk + i)),
                  _resident(conv_w.shape), _resident(conv_b.shape), _resident(a_log.shape),
                  _resident(a_log_t.shape), _resident(d_exp.shape), _resident(norm_w.shape)],
        out_specs=pl.BlockSpec((None, lb, d_inner), lambda b, i: (b, i, 0)),
        out_shape=jax.ShapeDtypeStruct((bsz, seq, d_inner), BF16),
        scratch_shapes=[pltpu.VMEM((SSM_GROUPS, SSM_DSTATE, gw), F32),
                        pltpu.VMEM((lb + 8, conv_dim), F32),
                        pltpu.VMEM((lb, d_inner), F32)],
        compiler_params=_params("parallel", "arbitrary"),
        name="ssd",
    )(x, dt, dtt, conv_w, conv_b, a_log, a_log_t, d_exp, norm_w)


def _att_kernel(x_ref, bias_ref, o_ref, k_ref, v_ref, *, width):
    seq = x_ref.shape[0]
    pad = ATT_LEFT_CHUNKS * CHUNK
    band = pad + CHUNK
    pair = 2 * ATT_HD
    k_ref[0:pad, :] = jnp.zeros((pad, width), BF16)
    v_ref[0:pad, :] = jnp.zeros((pad, width), BF16)
    k_ref[pad:pad + seq, :] = x_ref[:, width:2 * width]
    v_ref[pad:pad + seq, :] = x_ref[:, 2 * width:3 * width]
    lane = lax.broadcasted_iota(jnp.int32, (CHUNK, pair), 1)
    first = lane < ATT_HD
    key_pos = lax.broadcasted_iota(jnp.int32, (2 * CHUNK, band), 1)
    scale = ATT_HD ** -0.5

    def chunk_body(c, carry):
        start = pl.multiple_of(c * CHUNK, CHUNK)
        for p in range(width // pair):
            cols = slice(p * pair, (p + 1) * pair)
            q = x_ref[pl.ds(start, CHUNK), cols] * scale
            q2 = jnp.concatenate([jnp.where(first, q, 0), jnp.where(first, 0, q)], axis=0).astype(BF16)
            kb = k_ref[pl.ds(start, band), cols]
            vb = v_ref[pl.ds(start, band), cols]
            s = _dot_nt(q2, kb) + bias_ref[p]
            s = jnp.where(key_pos >= pad - start, s, -jnp.inf)
            e = jnp.exp(s - jnp.max(s, axis=-1, keepdims=True))
            prob = e * (1.0 / jnp.sum(e, axis=-1, keepdims=True))
            o2 = _dot(prob.astype(BF16), vb)
            o_ref[pl.ds(start, CHUNK), cols] = jnp.where(first, o2[:CHUNK], o2[CHUNK:]).astype(BF16)
        return carry

    lax.fori_loop(0, seq // CHUNK, chunk_body, 0)


def _att(x, bias, width):
    bsz, seq, _ = x.shape
    pad = ATT_LEFT_CHUNKS * CHUNK
    return pl.pallas_call(
        functools.partial(_att_kernel, width=width),
        grid=(bsz,),
        in_specs=[pl.BlockSpec((None, seq, 3 * width), lambda b: (b, 0, 0)), _resident(bias.shape)],
        out_specs=pl.BlockSpec((None, seq, width), lambda b: (b, 0, 0)),
        out_shape=jax.ShapeDtypeStruct((bsz, seq, width), BF16),
        scratch_shapes=[pltpu.VMEM((pad + seq, width), BF16), pltpu.VMEM((pad + seq, width), BF16)],
        compiler_params=_params("parallel"),
        name="att",
    )(x, bias)


def _att_bias(rel_table):
    pad = ATT_LEFT_CHUNKS * CHUNK
    band = pad + CHUNK
    rel = jnp.arange(CHUNK)[:, None] + pad - jnp.arange(band)[None, :]
    idx = jnp.clip(rel, -ATT_MAX_REL, ATT_MAX_REL) + ATT_MAX_REL
    bias = jnp.transpose(rel_table[idx], (2, 0, 1)).astype(F32)
    return bias.reshape(bias.shape[0] // 2, 2 * CHUNK, band)


def _merge_kernel(h_ref, a_ref, s_ref, c_ref, g_ref, wa_ref, ws_ref, wc_ref, wo_ref, o_ref):
    d = h_ref.shape[1]
    ya = _dot(a_ref[...], wa_ref[...])
    ys = _dot(s_ref[...], ws_ref[...])
    yc = _dot(c_ref[...], wc_ref[...])
    merged = (g_ref[:, 0:d].astype(F32) * ya + g_ref[:, d:2 * d].astype(F32) * ys
              + g_ref[:, 2 * d:3 * d].astype(F32) * yc)
    o_ref[...] = h_ref[...] + _dot(merged.astype(BF16), wo_ref[...])


def _merge(h, o_gla, o_ssm, o_att, gates, wa, ws, wc, wo):
    t, d = h.shape
    bm = min(TOKEN_BLOCK, t)

    def row(cols):
        return pl.BlockSpec((bm, cols), lambda i: (i, 0))

    return pl.pallas_call(
        _merge_kernel, grid=(t // bm,),
        in_specs=[row(d), row(o_gla.shape[1]), row(o_ssm.shape[1]), row(o_att.shape[1]), row(3 * d),
                  _resident(wa.shape), _resident(ws.shape), _resident(wc.shape), _resident(wo.shape)],
        out_specs=row(d),
        out_shape=jax.ShapeDtypeStruct((t, d), F32),
        compiler_params=_params("parallel"),
        name="merge",
    )(h, o_gla, o_ssm, o_att, gates, wa, ws, wc, wo)


def _mixer_layer(h, bsz, seq, mix_norm, w_in, gla_w_gk, gla_b_gk, gla_norm, ssm_conv_w, ssm_conv_b, ssm_dt_bias,
                 ssm_A_log, ssm_D, ssm_norm, attn_rel_bias, gate_bias, w_branch_gla, w_branch_ssm, w_branch_attn,
                 w_out):
    t, d = h.shape
    rank, gla_dk = gla_w_gk.shape
    gla_dv = w_branch_gla.shape[0]
    d_inner = w_branch_ssm.shape[0]
    conv_dim = ssm_conv_w.shape[1]
    heads = ssm_A_log.shape[0]
    att_w = w_branch_attn.shape[0]
    sizes = (gla_dk, gla_dk, gla_dv, gla_dv, rank, d_inner, conv_dim, heads, att_w, att_w, att_w, 3 * d)
    offs = [0]
    for s in sizes:
        offs.append(offs[-1] + s)
    w = w_in.astype(BF16)
    nw = mix_norm.reshape(1, d)

    qkvr, gk = _proj_call(
        _proj_gla_kernel, "proj_gla", h,
        [nw, w[:, offs[0]:offs[4]], w[:, offs[4]:offs[5]], gla_w_gk.astype(BF16), gla_b_gk.reshape(1, gla_dk)],
        [(offs[4], BF16, False), (gla_dk, F32, False)])
    w_dt = w[:, offs[7]:offs[8]]
    zx, dt, dtt = _proj_call(
        _proj_ssm_kernel, "proj_ssm", h,
        [nw, w[:, offs[5]:offs[7]], w_dt, w_dt.T, ssm_dt_bias.reshape(1, heads), ssm_dt_bias.reshape(heads, 1)],
        [(offs[7] - offs[5], BF16, False), (heads, F32, False), (heads, F32, True)])
    qkv = _proj_call(_proj_att_kernel, "proj_att", h, [nw, w[:, offs[8]:offs[11]]], [(3 * att_w, BF16, False)])[0]
    gates = _proj_call(_proj_gate_kernel, "proj_gate", h,
                       [nw, w[:, offs[11]:offs[12]], gate_bias.reshape(1, 3 * d)], [(3 * d, BF16, False)])[0]

    o_gla = _gla(qkvr.reshape(bsz, seq, -1), gk.reshape(bsz, seq, gla_dk),
                 gla_norm.reshape(1, -1), gla_dk, gla_dv)
    o_ssm = _ssd(zx.reshape(bsz, seq, -1), dt.reshape(bsz, seq, heads), dtt,
                 ssm_conv_w, ssm_conv_b.reshape(1, conv_dim), ssm_A_log.reshape(1, heads),
                 ssm_A_log.reshape(heads, 1), jnp.repeat(ssm_D, SSM_HEADDIM).reshape(1, d_inner),
                 ssm_norm.reshape(1, d_inner), d_inner)
    o_att = _att(qkv.reshape(bsz, seq, -1), _att_bias(attn_rel_bias), att_w)
    return _merge(h, o_gla.reshape(t, -1), o_ssm.reshape(t, -1), o_att.reshape(t, -1), gates,
                  w_branch_gla.astype(BF16), w_branch_ssm.astype(BF16), w_branch_attn.astype(BF16),
                  w_out.astype(BF16))


def kernel(x, ffn1_norm, ffn1_w_gate, ffn1_w_up, ffn1_w_down, mix_norm, w_in, gla_w_gk, gla_b_gk, gla_norm, ssm_conv_w, ssm_conv_b, ssm_dt_bias, ssm_A_log, ssm_D, ssm_norm, attn_rel_bias, gate_bias, w_branch_gla, w_branch_ssm, w_branch_attn, w_out, ffn2_norm, ffn2_w_gate, ffn2_w_up, ffn2_w_down, final_norm):
    bsz, seq, d = x.shape
    depth = w_in.shape[0]
    h = x.reshape(bsz * seq, d)
    fw = final_norm.reshape(1, d)
    for l in range(depth):
        h = _ffn(h, ffn1_norm[l].reshape(1, d), ffn1_w_gate[l].astype(BF16), ffn1_w_up[l].astype(BF16),
                 ffn1_w_down[l].astype(BF16), fw, False)
        h = _mixer_layer(h, bsz, seq, mix_norm[l], w_in[l], gla_w_gk[l], gla_b_gk[l], gla_norm[l], ssm_conv_w[l],
                         ssm_conv_b[l], ssm_dt_bias[l], ssm_A_log[l], ssm_D[l], ssm_norm[l], attn_rel_bias[l],
                         gate_bias[l], w_branch_gla[l], w_branch_ssm[l], w_branch_attn[l], w_out[l])
        h = _ffn(h, ffn2_norm[l].reshape(1, d), ffn2_w_gate[l].astype(BF16), ffn2_w_up[l].astype(BF16),
                 ffn2_w_down[l].astype(BF16), fw, l == depth - 1)
    return h.reshape(bsz, seq, d)
```

```python
import functools

import jax
import jax.numpy as jnp
from jax import lax
from jax.experimental import pallas as pl
from jax.experimental.pallas import tpu as pltpu

F32 = jnp.float32
BF16 = jnp.bfloat16

EPS = 1e-6
CHUNK = 64
GLA_HEADS = 4
GLA_GATE_NORM = 16.0
SSM_HEADDIM = 64
SSM_GROUPS = 8
SSM_DSTATE = 128
SSM_CONV = 4
ATT_HD = 64
ATT_LEFT_CHUNKS = 8
ATT_MAX_REL = 256

V7X_VMEM_BYTES = 64 * 1024 * 1024
VMEM_LIMIT_BYTES = V7X_VMEM_BYTES - 8 * 1024 * 1024

TOKEN_BLOCK = 512
SEQ_BLOCK = 256


def _params(*semantics):
    return pltpu.CompilerParams(dimension_semantics=semantics, vmem_limit_bytes=VMEM_LIMIT_BYTES)


def _resident(shape):
    zeros = (0,) * len(shape)
    return pl.BlockSpec(shape, lambda *_: zeros, pipeline_mode=pl.Buffered(1))


def _dot(a, b):
    return jnp.dot(a, b, preferred_element_type=F32)


def _dot_nt(a, b):
    return lax.dot_general(a, b, (((1,), (1,)), ((), ())), preferred_element_type=F32)


def _dot_tn(a, b):
    return lax.dot_general(a, b, (((0,), (0,)), ((), ())), preferred_element_type=F32)


def _rms(x, w):
    return x * lax.rsqrt(jnp.mean(x * x, axis=-1, keepdims=True) + EPS) * w


def _split3(x):
    hi = x.astype(BF16)
    r = x - hi.astype(F32)
    mid = r.astype(BF16)
    lo = (r - mid.astype(F32)).astype(BF16)
    return hi, mid, lo


def _sel_left(sel, x):
    hi, mid, lo = _split3(x)
    return _dot(sel, hi) + _dot(sel, mid) + _dot(sel, lo)


def _sel_right(x, sel):
    hi, mid, lo = _split3(x)
    return _dot(hi, sel) + _dot(mid, sel) + _dot(lo, sel)


def _tri(n, lower):
    r = lax.broadcasted_iota(jnp.int32, (n, n), 0)
    c = lax.broadcasted_iota(jnp.int32, (n, n), 1)
    return (r >= c) if lower else (r <= c)


def _ffn_kernel(h_ref, nw_ref, wg_ref, wu_ref, wd_ref, fw_ref, o_ref, *, f_chunks, final_norm):
    x = h_ref[...]
    xn = _rms(x, nw_ref[...]).astype(BF16)
    f_dim = wg_ref.shape[1]
    fc = f_dim // f_chunks
    acc = None
    for i in range(f_chunks):
        g = _dot(xn, wg_ref[:, i * fc:(i + 1) * fc])
        u = _dot(xn, wu_ref[:, i * fc:(i + 1) * fc])
        a = (g * jax.nn.sigmoid(g) * u).astype(BF16)
        y = _dot(a, wd_ref[i * fc:(i + 1) * fc, :])
        acc = y if acc is None else acc + y
    out = x + 0.5 * acc
    if final_norm:
        out = _rms(out, fw_ref[...])
    o_ref[...] = out


def _ffn(h, norm_w, wg, wu, wd, final_w, final_norm):
    t, d = h.shape
    f_dim = wg.shape[1]
    bm = min(TOKEN_BLOCK, t)
    row = pl.BlockSpec((bm, d), lambda i: (i, 0))
    return pl.pallas_call(
        functools.partial(_ffn_kernel, f_chunks=2, final_norm=final_norm),
        grid=(t // bm,),
        in_specs=[row, _resident((1, d)), _resident((d, f_dim)), _resident((d, f_dim)),
                  _resident((f_dim, d)), _resident((1, d))],
        out_specs=row,
        out_shape=jax.ShapeDtypeStruct((t, d), F32),
        compiler_params=_params("parallel"),
        name="ffn",
    )(h, norm_w, wg, wu, wd, final_w)


def _proj_gla_kernel(h_ref, nw_ref, w_ref, wc_ref, wgk_ref, bgk_ref, o_ref, gk_ref):
    xn = _rms(h_ref[...], nw_ref[...]).astype(BF16)
    o_ref[...] = _dot(xn, w_ref[...]).astype(BF16)
    code = _dot(xn, wc_ref[...]).astype(BF16)
    pre = _dot(code, wgk_ref[...]) + bgk_ref[...]
    gk_ref[...] = jax.nn.log_sigmoid(pre) * (1.0 / GLA_GATE_NORM)


def _proj_ssm_kernel(h_ref, nw_ref, w_ref, wdt_ref, dtb_ref, o_ref, dt_ref):
    xn = _rms(h_ref[...], nw_ref[...]).astype(BF16)
    o_ref[...] = _dot(xn, w_ref[...]).astype(BF16)
    dt_ref[...] = jax.nn.softplus(_dot(xn, wdt_ref[...]) + dtb_ref[...])


def _proj_att_kernel(h_ref, nw_ref, w_ref, o_ref):
    xn = _rms(h_ref[...], nw_ref[...]).astype(BF16)
    o_ref[...] = _dot(xn, w_ref[...]).astype(BF16)


def _proj_gate_kernel(h_ref, nw_ref, w_ref, b_ref, o_ref):
    xn = _rms(h_ref[...], nw_ref[...]).astype(BF16)
    o_ref[...] = jax.nn.sigmoid(_dot(xn, w_ref[...]) + b_ref[...]).astype(BF16)


def _proj_call(kernel, name, h, consts, outs):
    t, d = h.shape
    bm = min(TOKEN_BLOCK, t)
    in_specs = [pl.BlockSpec((bm, d), lambda i: (i, 0))] + [_resident(c.shape) for c in consts]
    out_specs = [pl.BlockSpec((bm, cols), lambda i: (i, 0)) for cols, _ in outs]
    out_shape = [jax.ShapeDtypeStruct((t, cols), dtype) for cols, dtype in outs]
    return pl.pallas_call(
        kernel, grid=(t // bm,), in_specs=in_specs, out_specs=out_specs, out_shape=out_shape,
        compiler_params=_params("parallel"), name=name,
    )(h, *consts)


def _gla_kernel(x_ref, gk_ref, nw_ref, o_ref, state_ref, *, dk, dv):
    hk, hv = dk // GLA_HEADS, dv // GLA_HEADS
    scale = hk ** -0.5

    @pl.when(pl.program_id(1) == 0)
    def _():
        state_ref[...] = jnp.zeros_like(state_ref)

    tril = _tri(CHUNK, True)
    tril_b = tril.astype(BF16)
    nw = nw_ref[...]
    for j in range(x_ref.shape[0] // CHUNK):
        rows = slice(j * CHUNK, (j + 1) * CHUNK)
        b_all = _sel_left(tril_b, gk_ref[rows, :])
        for h in range(GLA_HEADS):
            q = x_ref[rows, h * hk:(h + 1) * hk].astype(F32) * scale
            k = x_ref[rows, dk + h * hk:dk + (h + 1) * hk].astype(F32)
            v = x_ref[rows, 2 * dk + h * hv:2 * dk + (h + 1) * hv]
            r = x_ref[rows, 2 * dk + dv + h * hv:2 * dk + dv + (h + 1) * hv].astype(F32)
            b = b_all[:, h * hk:(h + 1) * hk]
            b_last = b[CHUNK - 1:CHUNK, :]
            b_mid = b[CHUNK // 2:CHUNK // 2 + 1, :]
            a = _dot_nt((q * jnp.exp(b - b_mid)).astype(BF16), (k * jnp.exp(b_mid - b)).astype(BF16))
            a = jnp.where(tril, a, 0.0)
            st = state_ref[h]
            o = _dot(a.astype(BF16), v) + _dot_nt((q * jnp.exp(b)).astype(BF16), st.astype(BF16))
            state_ref[h] = jnp.exp(b_last) * st + _dot_tn(v, (k * jnp.exp(b_last - b)).astype(BF16))
            o = _rms(o, nw) * (r * jax.nn.sigmoid(r))
            o_ref[rows, h * hv:(h + 1) * hv] = o.astype(BF16)


def _gla(x, gk, norm_w, dk, dv):
    bsz, seq, width = x.shape
    lb = min(SEQ_BLOCK, seq)
    hk, hv = dk // GLA_HEADS, dv // GLA_HEADS
    return pl.pallas_call(
        functools.partial(_gla_kernel, dk=dk, dv=dv),
        grid=(bsz, seq // lb),
        in_specs=[pl.BlockSpec((None, lb, width), lambda b, i: (b, i, 0)),
                  pl.BlockSpec((None, lb, dk), lambda b, i: (b, i, 0)),
                  _resident((1, hv))],
        out_specs=pl.BlockSpec((None, lb, dv), lambda b, i: (b, i, 0)),
        out_shape=jax.ShapeDtypeStruct((bsz, seq, dv), BF16),
        scratch_shapes=[pltpu.VMEM((GLA_HEADS, hv, hk), F32)],
        compiler_params=_params("parallel", "arbitrary"),
        name="gla",
    )(x, gk, norm_w)


def _ssd_kernel(x_ref, dt_ref, cw_ref, cb_ref, alog_ref, dexp_ref, nw_ref, tblk_ref, e3_ref, e2_ref,
                o_ref, state_ref, ext_ref, xs_ref, bc_ref, *, d_inner):
    lb = x_ref.shape[0]
    gn = SSM_GROUPS * SSM_DSTATE
    gw = d_inner // SSM_GROUPS
    hpg = gw // SSM_HEADDIM
    halo = 8

    @pl.when(pl.program_id(1) == 0)
    def _():
        state_ref[...] = jnp.zeros_like(state_ref)
        ext_ref[0:halo, :] = jnp.zeros((halo, ext_ref.shape[1]), F32)

    ext_ref[halo:halo + lb, :] = x_ref[:, d_inner:].astype(F32)
    conv = cb_ref[...] + cw_ref[SSM_CONV - 1:SSM_CONV, :] * ext_ref[halo:halo + lb, :]
    for tap in range(SSM_CONV - 1):
        off = halo - (SSM_CONV - 1) + tap
        conv = conv + cw_ref[tap:tap + 1, :] * ext_ref[off:off + lb, :]
    ext_ref[0:halo, :] = ext_ref[lb:lb + halo, :]
    xbc = conv * jax.nn.sigmoid(conv)
    xs_ref[...] = xbc[:, :d_inner]
    bc_ref[...] = xbc[:, d_inner:].astype(BF16)

    dt = dt_ref[...]
    cum = _sel_left(tblk_ref[...], dt * (-jnp.exp(alog_ref[...])))
    cum_exp = _dot(jnp.concatenate(_split3(cum), axis=1), e3_ref[...])
    dt_hi, dt_mid, _ = _split3(dt)
    dt_exp = _dot(jnp.concatenate([dt_hi, dt_mid], axis=1), e2_ref[...])

    row = lax.broadcasted_iota(jnp.int32, (CHUNK, d_inner), 0)
    key = lax.rem(lax.broadcasted_iota(jnp.int32, (CHUNK, d_inner), 1), SSM_HEADDIM)
    diag = row == key
    causal = (row >= key)[:, :gw]
    blockdiag = (lax.broadcasted_iota(jnp.int32, (gw, gw), 0) // CHUNK
                 == lax.broadcasted_iota(jnp.int32, (gw, gw), 1) // SSM_HEADDIM)
    for j in range(lb // CHUNK):
        rows = slice(j * CHUNK, (j + 1) * CHUNK)
        ce = cum_exp[rows, :]
        de = dt_exp[rows, :]
        cum_s = jnp.sum(jnp.where(diag, ce, 0.0), axis=0, keepdims=True)
        dt_s = jnp.sum(jnp.where(diag, de, 0.0), axis=0, keepdims=True)
        cum_last = ce[CHUNK - 1:CHUNK, :]
        for g in range(SSM_GROUPS):
            cols = slice(g * gw, (g + 1) * gw)
            bg = bc_ref[rows, g * SSM_DSTATE:(g + 1) * SSM_DSTATE]
            cg = bc_ref[rows, gn + g * SSM_DSTATE:gn + (g + 1) * SSM_DSTATE]
            cb = _dot_nt(cg, jnp.concatenate([bg, bg], axis=0))
            cb = jnp.concatenate([cb] * (hpg // 2), axis=1)
            ceg = ce[:, cols]
            lam = jnp.exp(jnp.where(causal, ceg - cum_s[:, cols], -jnp.inf))
            m = (cb * lam * dt_s[:, cols]).astype(BF16)
            xg = xs_ref[rows, cols]
            xbd = jnp.where(blockdiag, jnp.concatenate([xg] * hpg, axis=0), 0.0).astype(BF16)
            st = state_ref[g]
            y = _dot(m, xbd) + _dot(cg, st.astype(BF16)) * jnp.exp(ceg) + dexp_ref[:, cols] * xg
            w = de[:, cols] * jnp.exp(cum_last[:, cols] - ceg)
            state_ref[g] = jnp.exp(cum_last[:, cols]) * st + _dot_tn(bg, (xg * w).astype(BF16))
            z = x_ref[rows, cols].astype(F32)
            o_ref[rows, cols] = _rms(y * (z * jax.nn.sigmoid(z)), nw_ref[:, cols]).astype(BF16)


def _ssd(x, dt, conv_w, conv_b, a_log, d_exp, norm_w, d_inner):
    bsz, seq, width = x.shape
    heads = dt.shape[-1]
    assert CHUNK == SSM_HEADDIM and d_inner == heads * SSM_HEADDIM
    lb = min(SEQ_BLOCK, seq)
    conv_dim = width - d_inner
    gw = d_inner // SSM_GROUPS
    pos = jnp.arange(lb)
    tblk = ((pos[:, None] >= pos[None, :]) & (pos[:, None] // CHUNK == pos[None, :] // CHUNK)).astype(BF16)
    expand = (jnp.arange(heads)[:, None] == jnp.arange(d_inner)[None, :] // SSM_HEADDIM).astype(BF16)
    consts = [conv_w, conv_b, a_log, d_exp, norm_w, tblk, jnp.tile(expand, (3, 1)), jnp.tile(expand, (2, 1))]
    return pl.pallas_call(
        functools.partial(_ssd_kernel, d_inner=d_inner),
        grid=(bsz, seq // lb),
        in_specs=[pl.BlockSpec((None, lb, width), lambda b, i: (b, i, 0)),
                  pl.BlockSpec((None, lb, heads), lambda b, i: (b, i, 0))] + [_resident(c.shape) for c in consts],
        out_specs=pl.BlockSpec((None, lb, d_inner), lambda b, i: (b, i, 0)),
        out_shape=jax.ShapeDtypeStruct((bsz, seq, d_inner), BF16),
        scratch_shapes=[pltpu.VMEM((SSM_GROUPS, SSM_DSTATE, gw), F32),
                        pltpu.VMEM((lb + 8, conv_dim), F32),
                        pltpu.VMEM((lb, d_inner), F32),
                        pltpu.VMEM((lb, conv_dim - d_inner), BF16)],
        compiler_params=_params("parallel", "arbitrary"),
        name="ssd",
    )(x, dt, *consts)


def _att_kernel(x_ref, bias_ref, o_ref, k_ref, v_ref, *, width):
    seq = x_ref.shape[0]
    pad = ATT_LEFT_CHUNKS * CHUNK
    band = pad + CHUNK
    pair = 2 * ATT_HD
    k_ref[0:pad, :] = jnp.zeros((pad, width), BF16)
    v_ref[0:pad, :] = jnp.zeros((pad, width), BF16)
    k_ref[pad:pad + seq, :] = x_ref[:, width:2 * width]
    v_ref[pad:pad + seq, :] = x_ref[:, 2 * width:3 * width]
    lane = lax.broadcasted_iota(jnp.int32, (CHUNK, pair), 1)
    first = lane < ATT_HD
    key_pos = lax.broadcasted_iota(jnp.int32, (2 * CHUNK, band), 1)
    scale = ATT_HD ** -0.5

    def chunk_body(c, carry):
        start = pl.multiple_of(c * CHUNK, CHUNK)
        for p in range(width // pair):
            cols = slice(p * pair, (p + 1) * pair)
            q = x_ref[pl.ds(start, CHUNK), cols] * scale
            q2 = jnp.concatenate([jnp.where(first, q, 0), jnp.where(first, 0, q)], axis=0).astype(BF16)
            kb = k_ref[pl.ds(start, band), cols]
            vb = v_ref[pl.ds(start, band), cols]
            s = _dot_nt(q2, kb) + bias_ref[p]
            s = jnp.where(key_pos >= pad - start, s, -jnp.inf)
            e = jnp.exp(s - jnp.max(s, axis=-1, keepdims=True))
            prob = e * (1.0 / jnp.sum(e, axis=-1, keepdims=True))
            o2 = _dot(prob.astype(BF16), vb)
            o_ref[pl.ds(start, CHUNK), cols] = jnp.where(first, o2[:CHUNK], o2[CHUNK:]).astype(BF16)
        return carry

    lax.fori_loop(0, seq // CHUNK, chunk_body, 0)


def _att(x, bias, width):
    bsz, seq, _ = x.shape
    pad = ATT_LEFT_CHUNKS * CHUNK
    return pl.pallas_call(
        functools.partial(_att_kernel, width=width),
        grid=(bsz,),
        in_specs=[pl.BlockSpec((None, seq, 3 * width), lambda b: (b, 0, 0)), _resident(bias.shape)],
        out_specs=pl.BlockSpec((None, seq, width), lambda b: (b, 0, 0)),
        out_shape=jax.ShapeDtypeStruct((bsz, seq, width), BF16),
        scratch_shapes=[pltpu.VMEM((pad + seq, width), BF16), pltpu.VMEM((pad + seq, width), BF16)],
        compiler_params=_params("parallel"),
        name="att",
    )(x, bias)


def _att_bias(rel_table):
    pad = ATT_LEFT_CHUNKS * CHUNK
    band = pad + CHUNK
    rel = jnp.arange(CHUNK)[:, None] + pad - jnp.arange(band)[None, :]
    idx = jnp.clip(rel, -ATT_MAX_REL, ATT_MAX_REL) + ATT_MAX_REL
    bias = jnp.transpose(rel_table[idx], (2, 0, 1)).astype(F32)
    return bias.reshape(bias.shape[0] // 2, 2 * CHUNK, band)


def _merge_kernel(h_ref, a_ref, s_ref, c_ref, g_ref, wa_ref, ws_ref, wc_ref, wo_ref, o_ref):
    d = h_ref.shape[1]
    ya = _dot(a_ref[...], wa_ref[...])
    ys = _dot(s_ref[...], ws_ref[...])
    yc = _dot(c_ref[...], wc_ref[...])
    merged = (g_ref[:, 0:d].astype(F32) * ya + g_ref[:, d:2 * d].astype(F32) * ys
              + g_ref[:, 2 * d:3 * d].astype(F32) * yc)
    o_ref[...] = h_ref[...] + _dot(merged.astype(BF16), wo_ref[...])


def _merge(h, o_gla, o_ssm, o_att, gates, wa, ws, wc, wo):
    t, d = h.shape
    bm = min(TOKEN_BLOCK, t)

    def row(cols):
        return pl.BlockSpec((bm, cols), lambda i: (i, 0))

    return pl.pallas_call(
        _merge_kernel, grid=(t // bm,),
        in_specs=[row(d), row(o_gla.shape[1]), row(o_ssm.shape[1]), row(o_att.shape[1]), row(3 * d),
                  _resident(wa.shape), _resident(ws.shape), _resident(wc.shape), _resident(wo.shape)],
        out_specs=row(d),
        out_shape=jax.ShapeDtypeStruct((t, d), F32),
        compiler_params=_params("parallel"),
        name="merge",
    )(h, o_gla, o_ssm, o_att, gates, wa, ws, wc, wo)


def _mixer_layer(h, bsz, seq, mix_norm, w_in, gla_w_gk, gla_b_gk, gla_norm, ssm_conv_w, ssm_conv_b, ssm_dt_bias,
                 ssm_A_log, ssm_D, ssm_norm, attn_rel_bias, gate_bias, w_branch_gla, w_branch_ssm, w_branch_attn,
                 w_out):
    t, d = h.shape
    rank, gla_dk = gla_w_gk.shape
    gla_dv = w_branch_gla.shape[0]
    d_inner = w_branch_ssm.shape[0]
    conv_dim = ssm_conv_w.shape[1]
    heads = ssm_A_log.shape[0]
    att_w = w_branch_attn.shape[0]
    sizes = (gla_dk, gla_dk, gla_dv, gla_dv, rank, d_inner, conv_dim, heads, att_w, att_w, att_w, 3 * d)
    offs = [0]
    for s in sizes:
        offs.append(offs[-1] + s)
    w = w_in.astype(BF16)
    nw = mix_norm.reshape(1, d)

    qkvr, gk = _proj_call(
        _proj_gla_kernel, "proj_gla", h,
        [nw, w[:, offs[0]:offs[4]], w[:, offs[4]:offs[5]], gla_w_gk.astype(BF16), gla_b_gk.reshape(1, gla_dk)],
        [(offs[4], BF16), (gla_dk, F32)])
    zx, dt = _proj_call(
        _proj_ssm_kernel, "proj_ssm", h,
        [nw, w[:, offs[5]:offs[7]], w[:, offs[7]:offs[8]], ssm_dt_bias.reshape(1, heads)],
        [(offs[7] - offs[5], BF16), (heads, F32)])
    qkv = _proj_call(_proj_att_kernel, "proj_att", h, [nw, w[:, offs[8]:offs[11]]], [(3 * att_w, BF16)])[0]
    gates = _proj_call(_proj_gate_kernel, "proj_gate", h,
                       [nw, w[:, offs[11]:offs[12]], gate_bias.reshape(1, 3 * d)], [(3 * d, BF16)])[0]

    o_gla = _gla(qkvr.reshape(bsz, seq, -1), gk.reshape(bsz, seq, gla_dk),
                 gla_norm.reshape(1, -1), gla_dk, gla_dv)
    o_ssm = _ssd(zx.reshape(bsz, seq, -1), dt.reshape(bsz, seq, heads),
                 ssm_conv_w, ssm_conv_b.reshape(1, conv_dim), ssm_A_log.reshape(1, heads),
                 jnp.repeat(ssm_D, SSM_HEADDIM).reshape(1, d_inner), ssm_norm.reshape(1, d_inner), d_inner)
    o_att = _att(qkv.reshape(bsz, seq, -1), _att_bias(attn_rel_bias), att_w)
    return _merge(h, o_gla.reshape(t, -1), o_ssm.reshape(t, -1), o_att.reshape(t, -1), gates,
                  w_branch_gla.astype(BF16), w_branch_ssm.astype(BF16), w_branch_attn.astype(BF16),
                  w_out.astype(BF16))


def kernel(x, ffn1_norm, ffn1_w_gate, ffn1_w_up, ffn1_w_down, mix_norm, w_in, gla_w_gk, gla_b_gk, gla_norm, ssm_conv_w, ssm_conv_b, ssm_dt_bias, ssm_A_log, ssm_D, ssm_norm, attn_rel_bias, gate_bias, w_branch_gla, w_branch_ssm, w_branch_attn, w_out, ffn2_norm, ffn2_w_gate, ffn2_w_up, ffn2_w_down, final_norm):
    bsz, seq, d = x.shape
    depth = w_in.shape[0]
    h = x.reshape(bsz * seq, d)
    fw = final_norm.reshape(1, d)
    for l in range(depth):
        h = _ffn(h, ffn1_norm[l].reshape(1, d), ffn1_w_gate[l].astype(BF16), ffn1_w_up[l].astype(BF16),
                 ffn1_w_down[l].astype(BF16), fw, False)
        h = _mixer_layer(h, bsz, seq, mix_norm[l], w_in[l], gla_w_gk[l], gla_b_gk[l], gla_norm[l], ssm_conv_w[l],
                         ssm_conv_b[l], ssm_dt_bias[l], ssm_A_log[l], ssm_D[l], ssm_norm[l], attn_rel_bias[l],
                         gate_bias[l], w_branch_gla[l], w_branch_ssm[l], w_branch_attn[l], w_out[l])
        h = _ffn(h, ffn2_norm[l].reshape(1, d), ffn2_w_gate[l].astype(BF16), ffn2_w_up[l].astype(BF16),
                 ffn2_w_down[l].astype(BF16), fw, l == depth - 1)
    return h.reshape(bsz, seq, d)
```

```python
import functools

import jax
import jax.numpy as jnp
from jax import lax
from jax.experimental import pallas as pl
from jax.experimental.pallas import tpu as pltpu

F32 = jnp.float32
BF16 = jnp.bfloat16

EPS = 1e-6
CHUNK = 64
GLA_HEADS = 4
GLA_GATE_NORM = 16.0
SSM_HEADDIM = 64
SSM_GROUPS = 8
SSM_DSTATE = 128
SSM_CONV = 4
ATT_HD = 64
ATT_LEFT_CHUNKS = 8
ATT_MAX_REL = 256

V7X_VMEM_BYTES = 64 * 1024 * 1024
VMEM_LIMIT_BYTES = V7X_VMEM_BYTES - 8 * 1024 * 1024

TOKEN_BLOCK = 512
SEQ_BLOCK = 256


def _params(*semantics):
    return pltpu.CompilerParams(dimension_semantics=semantics, vmem_limit_bytes=VMEM_LIMIT_BYTES)


def _resident(shape):
    zeros = (0,) * len(shape)
    return pl.BlockSpec(shape, lambda *_: zeros, pipeline_mode=pl.Buffered(1))


def _dot(a, b):
    return jnp.dot(a, b, preferred_element_type=F32)


def _dot_nt(a, b):
    return lax.dot_general(a, b, (((1,), (1,)), ((), ())), preferred_element_type=F32)


def _dot_tn(a, b):
    return lax.dot_general(a, b, (((0,), (0,)), ((), ())), preferred_element_type=F32)


def _rms(x, w):
    return x * lax.rsqrt(jnp.mean(x * x, axis=-1, keepdims=True) + EPS) * w


def _split3(x):
    hi = x.astype(BF16)
    r = x - hi.astype(F32)
    mid = r.astype(BF16)
    lo = (r - mid.astype(F32)).astype(BF16)
    return hi, mid, lo


def _sel_left(sel, x):
    hi, mid, lo = _split3(x)
    return _dot(sel, hi) + _dot(sel, mid) + _dot(sel, lo)


def _sel_right(x, sel):
    hi, mid, lo = _split3(x)
    return _dot(hi, sel) + _dot(mid, sel) + _dot(lo, sel)


def _tri(n, lower):
    r = lax.broadcasted_iota(jnp.int32, (n, n), 0)
    c = lax.broadcasted_iota(jnp.int32, (n, n), 1)
    return (r >= c) if lower else (r <= c)


def _ffn_kernel(h_ref, nw_ref, wg_ref, wu_ref, wd_ref, fw_ref, o_ref, *, f_chunks, final_norm):
    x = h_ref[...]
    xn = _rms(x, nw_ref[...]).astype(BF16)
    f_dim = wg_ref.shape[1]
    fc = f_dim // f_chunks
    acc = None
    for i in range(f_chunks):
        g = _dot(xn, wg_ref[:, i * fc:(i + 1) * fc])
        u = _dot(xn, wu_ref[:, i * fc:(i + 1) * fc])
        a = (g * jax.nn.sigmoid(g) * u).astype(BF16)
        y = _dot(a, wd_ref[i * fc:(i + 1) * fc, :])
        acc = y if acc is None else acc + y
    out = x + 0.5 * acc
    if final_norm:
        out = _rms(out, fw_ref[...])
    o_ref[...] = out


def _ffn(h, norm_w, wg, wu, wd, final_w, final_norm):
    t, d = h.shape
    f_dim = wg.shape[1]
    bm = min(TOKEN_BLOCK, t)
    row = pl.BlockSpec((bm, d), lambda i: (i, 0))
    return pl.pallas_call(
        functools.partial(_ffn_kernel, f_chunks=2, final_norm=final_norm),
        grid=(t // bm,),
        in_specs=[row, _resident((1, d)), _resident((d, f_dim)), _resident((d, f_dim)),
                  _resident((f_dim, d)), _resident((1, d))],
        out_specs=row,
        out_shape=jax.ShapeDtypeStruct((t, d), F32),
        compiler_params=_params("parallel"),
        name="ffn",
    )(h, norm_w, wg, wu, wd, final_w)


def _proj_gla_kernel(h_ref, nw_ref, w_ref, wc_ref, wgk_ref, bgk_ref, o_ref, gk_ref):
    xn = _rms(h_ref[...], nw_ref[...]).astype(BF16)
    o_ref[...] = _dot(xn, w_ref[...]).astype(BF16)
    code = _dot(xn, wc_ref[...]).astype(BF16)
    pre = _dot(code, wgk_ref[...]) + bgk_ref[...]
    gk_ref[...] = jax.nn.log_sigmoid(pre) * (1.0 / GLA_GATE_NORM)


LANES = 128
CONV_HALO = 8


CONV_COL_TILE = 512
CONV_ROW_TILE = 64


def _proj_ssm_kernel(h_ref, nw_ref, w_ref, wdt_ref, dtb_ref, cw_ref, cb_ref, o_ref, dt_ref, carry_ref, acc_ref,
                     *, d_inner, blocks_per_seq):
    assert SSM_CONV == 4
    bm = h_ref.shape[0]

    @pl.when(pl.program_id(0) % blocks_per_seq == 0)
    def _():
        carry_ref[...] = jnp.zeros_like(carry_ref)

    xn = _rms(h_ref[...], nw_ref[...]).astype(BF16)
    dt_ref[...] = jax.nn.softplus(_dot(xn, wdt_ref[...]) + dtb_ref[...])
    conv_dim = w_ref.shape[1] - d_inner
    n_tiles = conv_dim // CONV_COL_TILE

    def matmul_tile(k):
        c0 = k * CONV_COL_TILE
        acc_ref[k % 2, 0:CONV_HALO, :] = carry_ref[:, c0:c0 + CONV_COL_TILE]
        acc_ref[k % 2, CONV_HALO:, :] = _dot(xn, w_ref[:, d_inner + c0:d_inner + c0 + CONV_COL_TILE])

    def conv_tile(k):
        c0 = k * CONV_COL_TILE
        for r0 in range(0, bm, CONV_ROW_TILE):
            for l0 in range(0, CONV_COL_TILE, LANES):
                cols = slice(c0 + l0, c0 + l0 + LANES)
                x0 = acc_ref[k % 2, r0:r0 + CONV_HALO + CONV_ROW_TILE, l0:l0 + LANES]
                x1 = pltpu.roll(x0, 1, axis=0)
                near = cw_ref[3:4, cols] * x0 + cw_ref[2:3, cols] * x1
                far = pltpu.roll(cw_ref[1:2, cols] * x0 + cw_ref[0:1, cols] * x1, 2, axis=0)
                conv = (cb_ref[:, cols] + near + far)[CONV_HALO:, :]
                o_ref[r0:r0 + CONV_ROW_TILE, d_inner + c0 + l0:d_inner + c0 + l0 + LANES] = (
                    conv * jax.nn.sigmoid(conv)).astype(BF16)
        carry_ref[:, c0:c0 + CONV_COL_TILE] = acc_ref[k % 2, bm:bm + CONV_HALO, :]

    matmul_tile(0)
    for k in range(n_tiles):
        if k + 1 < n_tiles:
            matmul_tile(k + 1)
        else:
            o_ref[:, :d_inner] = _dot(xn, w_ref[:, :d_inner]).astype(BF16)
        conv_tile(k)


def _proj_att_kernel(h_ref, nw_ref, w_ref, o_ref):
    xn = _rms(h_ref[...], nw_ref[...]).astype(BF16)
    o_ref[...] = _dot(xn, w_ref[...]).astype(BF16)


def _proj_gate_kernel(h_ref, nw_ref, w_ref, b_ref, o_ref):
    xn = _rms(h_ref[...], nw_ref[...]).astype(BF16)
    o_ref[...] = jax.nn.sigmoid(_dot(xn, w_ref[...]) + b_ref[...]).astype(BF16)


def _proj_call(kernel, name, h, consts, outs, scratch=(), carries=False):
    t, d = h.shape
    bm = min(TOKEN_BLOCK, t)
    in_specs = [pl.BlockSpec((bm, d), lambda i: (i, 0))] + [_resident(c.shape) for c in consts]
    out_specs = [pl.BlockSpec((bm, cols), lambda i: (i, 0)) for cols, _ in outs]
    out_shape = [jax.ShapeDtypeStruct((t, cols), dtype) for cols, dtype in outs]
    return pl.pallas_call(
        kernel, grid=(t // bm,), in_specs=in_specs, out_specs=out_specs, out_shape=out_shape,
        scratch_shapes=list(scratch),
        compiler_params=_params("arbitrary" if carries else "parallel"), name=name,
    )(h, *consts)


def _gla_kernel(x_ref, gk_ref, nw_ref, o_ref, state_ref, *, dk, dv):
    hk, hv = dk // GLA_HEADS, dv // GLA_HEADS
    scale = hk ** -0.5

    @pl.when(pl.program_id(1) == 0)
    def _():
        state_ref[...] = jnp.zeros_like(state_ref)

    tril = _tri(CHUNK, True)
    tril_b = tril.astype(BF16)
    nw = nw_ref[...]
    for j in range(x_ref.shape[0] // CHUNK):
        rows = slice(j * CHUNK, (j + 1) * CHUNK)
        b_all = _sel_left(tril_b, gk_ref[rows, :])
        for h in range(GLA_HEADS):
            q = x_ref[rows, h * hk:(h + 1) * hk].astype(F32) * scale
            k = x_ref[rows, dk + h * hk:dk + (h + 1) * hk].astype(F32)
            v = x_ref[rows, 2 * dk + h * hv:2 * dk + (h + 1) * hv]
            r = x_ref[rows, 2 * dk + dv + h * hv:2 * dk + dv + (h + 1) * hv].astype(F32)
            b = b_all[:, h * hk:(h + 1) * hk]
            b_last = b[CHUNK - 1:CHUNK, :]
            b_mid = b[CHUNK // 2:CHUNK // 2 + 1, :]
            a = _dot_nt((q * jnp.exp(b - b_mid)).astype(BF16), (k * jnp.exp(b_mid - b)).astype(BF16))
            a = jnp.where(tril, a, 0.0)
            st = state_ref[h]
            o = _dot(a.astype(BF16), v) + _dot_nt((q * jnp.exp(b)).astype(BF16), st.astype(BF16))
            state_ref[h] = jnp.exp(b_last) * st + _dot_tn(v, (k * jnp.exp(b_last - b)).astype(BF16))
            o = _rms(o, nw) * (r * jax.nn.sigmoid(r))
            o_ref[rows, h * hv:(h + 1) * hv] = o.astype(BF16)


def _gla(x, gk, norm_w, dk, dv):
    bsz, seq, width = x.shape
    lb = min(SEQ_BLOCK, seq)
    hk, hv = dk // GLA_HEADS, dv // GLA_HEADS
    return pl.pallas_call(
        functools.partial(_gla_kernel, dk=dk, dv=dv),
        grid=(bsz, seq // lb),
        in_specs=[pl.BlockSpec((None, lb, width), lambda b, i: (b, i, 0)),
                  pl.BlockSpec((None, lb, dk), lambda b, i: (b, i, 0)),
                  _resident((1, hv))],
        out_specs=pl.BlockSpec((None, lb, dv), lambda b, i: (b, i, 0)),
        out_shape=jax.ShapeDtypeStruct((bsz, seq, dv), BF16),
        scratch_shapes=[pltpu.VMEM((GLA_HEADS, hv, hk), F32)],
        compiler_params=_params("parallel", "arbitrary"),
        name="gla",
    )(x, gk, norm_w)


def _ssd_kernel(x_ref, dt_ref, alog_ref, dexp_ref, nw_ref, tblk_ref, e3_ref, e2_ref,
                o_ref, state_ref, *, d_inner):
    lb = x_ref.shape[0]
    gn = SSM_GROUPS * SSM_DSTATE
    gw = d_inner // SSM_GROUPS
    hpg = gw // SSM_HEADDIM
    b_off, c_off = 2 * d_inner, 2 * d_inner + gn

    @pl.when(pl.program_id(1) == 0)
    def _():
        state_ref[...] = jnp.zeros_like(state_ref)

    dt = dt_ref[...]
    cum = _sel_left(tblk_ref[...], dt * (-jnp.exp(alog_ref[...])))
    cum_exp = _dot(jnp.concatenate(_split3(cum), axis=1), e3_ref[...])
    dt_hi, dt_mid, _ = _split3(dt)
    dt_exp = _dot(jnp.concatenate([dt_hi, dt_mid], axis=1), e2_ref[...])

    row = lax.broadcasted_iota(jnp.int32, (CHUNK, d_inner), 0)
    key = lax.rem(lax.broadcasted_iota(jnp.int32, (CHUNK, d_inner), 1), SSM_HEADDIM)
    diag = row == key
    causal = (row >= key)[:, :gw]
    blockdiag = (lax.broadcasted_iota(jnp.int32, (gw, gw), 0) // CHUNK
                 == lax.broadcasted_iota(jnp.int32, (gw, gw), 1) // SSM_HEADDIM)
    for j in range(lb // CHUNK):
        rows = slice(j * CHUNK, (j + 1) * CHUNK)
        ce = cum_exp[rows, :]
        de = dt_exp[rows, :]
        cum_s = jnp.sum(jnp.where(diag, ce, 0.0), axis=0, keepdims=True)
        dt_s = jnp.sum(jnp.where(diag, de, 0.0), axis=0, keepdims=True)
        cum_last = ce[CHUNK - 1:CHUNK, :]
        for g in range(SSM_GROUPS):
            cols = slice(g * gw, (g + 1) * gw)
            bg = x_ref[rows, b_off + g * SSM_DSTATE:b_off + (g + 1) * SSM_DSTATE]
            cg = x_ref[rows, c_off + g * SSM_DSTATE:c_off + (g + 1) * SSM_DSTATE]
            cb = _dot_nt(cg, jnp.concatenate([bg, bg], axis=0))
            cb = jnp.concatenate([cb] * (hpg // 2), axis=1)
            ceg = ce[:, cols]
            lam = jnp.exp(jnp.where(causal, ceg - cum_s[:, cols], -jnp.inf))
            m = (cb * lam * dt_s[:, cols]).astype(BF16)
            xg_b = x_ref[rows, d_inner + g * gw:d_inner + (g + 1) * gw]
            xg = xg_b.astype(F32)
            xbd = jnp.where(blockdiag, jnp.concatenate([xg] * hpg, axis=0), 0.0).astype(BF16)
            st = state_ref[g]
            y = _dot(m, xbd) + _dot(cg, st.astype(BF16)) * jnp.exp(ceg) + dexp_ref[:, cols] * xg
            w = de[:, cols] * jnp.exp(cum_last[:, cols] - ceg)
            state_ref[g] = jnp.exp(cum_last[:, cols]) * st + _dot_tn(bg, (xg * w).astype(BF16))
            z = x_ref[rows, cols].astype(F32)
            o_ref[rows, cols] = _rms(y * (z * jax.nn.sigmoid(z)), nw_ref[:, cols]).astype(BF16)


def _ssd(x, dt, a_log, d_exp, norm_w, d_inner):
    bsz, seq, width = x.shape
    heads = dt.shape[-1]
    assert CHUNK == SSM_HEADDIM and d_inner == heads * SSM_HEADDIM
    lb = min(SEQ_BLOCK, seq)
    gw = d_inner // SSM_GROUPS
    pos = jnp.arange(lb)
    tblk = ((pos[:, None] >= pos[None, :]) & (pos[:, None] // CHUNK == pos[None, :] // CHUNK)).astype(BF16)
    expand = (jnp.arange(heads)[:, None] == jnp.arange(d_inner)[None, :] // SSM_HEADDIM).astype(BF16)
    consts = [a_log, d_exp, norm_w, tblk, jnp.tile(expand, (3, 1)), jnp.tile(expand, (2, 1))]
    return pl.pallas_call(
        functools.partial(_ssd_kernel, d_inner=d_inner),
        grid=(bsz, seq // lb),
        in_specs=[pl.BlockSpec((None, lb, width), lambda b, i: (b, i, 0)),
                  pl.BlockSpec((None, lb, heads), lambda b, i: (b, i, 0))] + [_resident(c.shape) for c in consts],
        out_specs=pl.BlockSpec((None, lb, d_inner), lambda b, i: (b, i, 0)),
        out_shape=jax.ShapeDtypeStruct((bsz, seq, d_inner), BF16),
        scratch_shapes=[pltpu.VMEM((SSM_GROUPS, SSM_DSTATE, gw), F32)],
        compiler_params=_params("parallel", "arbitrary"),
        name="ssd",
    )(x, dt, *consts)


ATT_Q_CHUNKS = 4
ATT_HEAD_GROUP = 4


def _att_kernel(x_ref, bias_ref, o_ref, *, width):
    seq = x_ref.shape[0]
    pad = ATT_LEFT_CHUNKS * CHUNK
    qrows = ATT_Q_CHUNKS * CHUNK
    win = pad + qrows
    gwid = ATT_HEAD_GROUP * ATT_HD
    scale = ATT_HD ** -0.5
    head_of_lane = lax.broadcasted_iota(jnp.int32, (qrows, gwid), 1) // ATT_HD

    def block(r0, k0, nk):
        for hq in range(width // gwid):
            cols = slice(hq * gwid, (hq + 1) * gwid)
            q = x_ref[pl.ds(r0, qrows), cols] * scale
            qm = jnp.concatenate([jnp.where(head_of_lane == h, q, 0) for h in range(ATT_HEAD_GROUP)], axis=0)
            kb = x_ref[pl.ds(k0, nk), width + hq * gwid:width + (hq + 1) * gwid]
            vb = x_ref[pl.ds(k0, nk), 2 * width + hq * gwid:2 * width + (hq + 1) * gwid]
            s = _dot_nt(qm.astype(BF16), kb)
            probs = []
            for h in range(ATT_HEAD_GROUP):
                sh = s[h * qrows:(h + 1) * qrows, :] + bias_ref[hq * ATT_HEAD_GROUP + h, :, win - nk:win]
                e = jnp.exp(sh - jnp.max(sh, axis=-1, keepdims=True))
                probs.append((e * (1.0 / jnp.sum(e, axis=-1, keepdims=True))).astype(BF16))
            o4 = _dot(jnp.concatenate(probs, axis=0), vb)
            o = o4[(ATT_HEAD_GROUP - 1) * qrows:, :]
            for h in range(ATT_HEAD_GROUP - 2, -1, -1):
                o = jnp.where(head_of_lane == h, o4[h * qrows:(h + 1) * qrows, :], o)
            o_ref[pl.ds(r0, qrows), cols] = o.astype(BF16)

    n_head = pad // qrows
    for i in range(n_head):
        block(i * qrows, 0, (i + 1) * qrows)

    def body(i, carry):
        r0 = pl.multiple_of(i * qrows, qrows)
        block(r0, pl.multiple_of(r0 - pad, qrows), win)
        return carry

    lax.fori_loop(n_head, seq // qrows, body, 0)


def _att(x, bias, width):
    bsz, seq, _ = x.shape
    assert seq % (ATT_Q_CHUNKS * CHUNK) == 0 and (ATT_LEFT_CHUNKS * CHUNK) % (ATT_Q_CHUNKS * CHUNK) == 0
    assert width % (ATT_HEAD_GROUP * ATT_HD) == 0
    return pl.pallas_call(
        functools.partial(_att_kernel, width=width),
        grid=(bsz,),
        in_specs=[pl.BlockSpec((None, seq, 3 * width), lambda b: (b, 0, 0)), _resident(bias.shape)],
        out_specs=pl.BlockSpec((None, seq, width), lambda b: (b, 0, 0)),
        out_shape=jax.ShapeDtypeStruct((bsz, seq, width), BF16),
        compiler_params=_params("parallel"),
        name="att",
    )(x, bias)


def _att_bias(rel_table):
    pad = ATT_LEFT_CHUNKS * CHUNK
    qrows = ATT_Q_CHUNKS * CHUNK
    win = pad + qrows
    n = qrows + win
    m = jnp.arange(n)
    off = jnp.where(m < win, m, m - n)
    u = rel_table[jnp.clip(pad - off, -ATT_MAX_REL, ATT_MAX_REL) + ATT_MAX_REL].astype(F32).T
    toep = jnp.tile(u, (1, qrows))[:, :qrows * (n - 1)].reshape(-1, qrows, n - 1)[:, :, :win]
    qc = jnp.arange(qrows)[:, None] // CHUNK
    kc = jnp.arange(win)[None, :] // CHUNK
    return jnp.where((kc >= qc) & (kc <= qc + ATT_LEFT_CHUNKS), toep, -jnp.inf)


def _merge_kernel(h_ref, a_ref, s_ref, c_ref, g_ref, wa_ref, ws_ref, wc_ref, wo_ref, o_ref):
    d = h_ref.shape[1]
    ya = _dot(a_ref[...], wa_ref[...])
    ys = _dot(s_ref[...], ws_ref[...])
    yc = _dot(c_ref[...], wc_ref[...])
    merged = (g_ref[:, 0:d].astype(F32) * ya + g_ref[:, d:2 * d].astype(F32) * ys
              + g_ref[:, 2 * d:3 * d].astype(F32) * yc)
    o_ref[...] = h_ref[...] + _dot(merged.astype(BF16), wo_ref[...])


def _merge(h, o_gla, o_ssm, o_att, gates, wa, ws, wc, wo):
    t, d = h.shape
    bm = min(TOKEN_BLOCK, t)

    def row(cols):
        return pl.BlockSpec((bm, cols), lambda i: (i, 0))

    return pl.pallas_call(
        _merge_kernel, grid=(t // bm,),
        in_specs=[row(d), row(o_gla.shape[1]), row(o_ssm.shape[1]), row(o_att.shape[1]), row(3 * d),
                  _resident(wa.shape), _resident(ws.shape), _resident(wc.shape), _resident(wo.shape)],
        out_specs=row(d),
        out_shape=jax.ShapeDtypeStruct((t, d), F32),
        compiler_params=_params("parallel"),
        name="merge",
    )(h, o_gla, o_ssm, o_att, gates, wa, ws, wc, wo)


def _mixer_layer(h, bsz, seq, mix_norm, w_in, gla_w_gk, gla_b_gk, gla_norm, ssm_conv_w, ssm_conv_b, ssm_dt_bias,
                 ssm_A_log, ssm_D, ssm_norm, attn_rel_bias, gate_bias, w_branch_gla, w_branch_ssm, w_branch_attn,
                 w_out):
    t, d = h.shape
    rank, gla_dk = gla_w_gk.shape
    gla_dv = w_branch_gla.shape[0]
    d_inner = w_branch_ssm.shape[0]
    conv_dim = ssm_conv_w.shape[1]
    heads = ssm_A_log.shape[0]
    att_w = w_branch_attn.shape[0]
    sizes = (gla_dk, gla_dk, gla_dv, gla_dv, rank, d_inner, conv_dim, heads, att_w, att_w, att_w, 3 * d)
    offs = [0]
    for s in sizes:
        offs.append(offs[-1] + s)
    w = w_in.astype(BF16)
    nw = mix_norm.reshape(1, d)

    qkvr, gk = _proj_call(
        _proj_gla_kernel, "proj_gla", h,
        [nw, w[:, offs[0]:offs[4]], w[:, offs[4]:offs[5]], gla_w_gk.astype(BF16), gla_b_gk.reshape(1, gla_dk)],
        [(offs[4], BF16), (gla_dk, F32)])
    zx, dt = _proj_call(
        functools.partial(_proj_ssm_kernel, d_inner=d_inner, blocks_per_seq=seq // min(TOKEN_BLOCK, seq)),
        "proj_ssm", h,
        [nw, w[:, offs[5]:offs[7]], w[:, offs[7]:offs[8]], ssm_dt_bias.reshape(1, heads),
         ssm_conv_w, ssm_conv_b.reshape(1, conv_dim)],
        [(offs[7] - offs[5], BF16), (heads, F32)],
        scratch=[pltpu.VMEM((CONV_HALO, conv_dim), F32),
                 pltpu.VMEM((2, CONV_HALO + min(TOKEN_BLOCK, seq), CONV_COL_TILE), F32)], carries=True)
    qkv = _proj_call(_proj_att_kernel, "proj_att", h, [nw, w[:, offs[8]:offs[11]]], [(3 * att_w, BF16)])[0]
    gates = _proj_call(_proj_gate_kernel, "proj_gate", h,
                       [nw, w[:, offs[11]:offs[12]], gate_bias.reshape(1, 3 * d)], [(3 * d, BF16)])[0]

    o_gla = _gla(qkvr.reshape(bsz, seq, -1), gk.reshape(bsz, seq, gla_dk),
                 gla_norm.reshape(1, -1), gla_dk, gla_dv)
    o_ssm = _ssd(zx.reshape(bsz, seq, -1), dt.reshape(bsz, seq, heads), ssm_A_log.reshape(1, heads),
                 jnp.repeat(ssm_D, SSM_HEADDIM).reshape(1, d_inner), ssm_norm.reshape(1, d_inner), d_inner)
    o_att = _att(qkv.reshape(bsz, seq, -1), _att_bias(attn_rel_bias), att_w)
    return _merge(h, o_gla.reshape(t, -1), o_ssm.reshape(t, -1), o_att.reshape(t, -1), gates,
                  w_branch_gla.astype(BF16), w_branch_ssm.astype(BF16), w_branch_attn.astype(BF16),
                  w_out.astype(BF16))


def kernel(x, ffn1_norm, ffn1_w_gate, ffn1_w_up, ffn1_w_down, mix_norm, w_in, gla_w_gk, gla_b_gk, gla_norm, ssm_conv_w, ssm_conv_b, ssm_dt_bias, ssm_A_log, ssm_D, ssm_norm, attn_rel_bias, gate_bias, w_branch_gla, w_branch_ssm, w_branch_attn, w_out, ffn2_norm, ffn2_w_gate, ffn2_w_up, ffn2_w_down, final_norm):
    bsz, seq, d = x.shape
    depth = w_in.shape[0]
    h = x.reshape(bsz * seq, d)
    fw = final_norm.reshape(1, d)
    for l in range(depth):
        h = _ffn(h, ffn1_norm[l].reshape(1, d), ffn1_w_gate[l].astype(BF16), ffn1_w_up[l].astype(BF16),
                 ffn1_w_down[l].astype(BF16), fw, False)
        h = _mixer_layer(h, bsz, seq, mix_norm[l], w_in[l], gla_w_gk[l], gla_b_gk[l], gla_norm[l], ssm_conv_w[l],
                         ssm_conv_b[l], ssm_dt_bias[l], ssm_A_log[l], ssm_D[l], ssm_norm[l], attn_rel_bias[l],
                         gate_bias[l], w_branch_gla[l], w_branch_ssm[l], w_branch_attn[l], w_out[l])
        h = _ffn(h, ffn2_norm[l].reshape(1, d), ffn2_w_gate[l].astype(BF16), ffn2_w_up[l].astype(BF16),
                 ffn2_w_down[l].astype(BF16), fw, l == depth - 1)
    return h.reshape(bsz, seq, d)
```

```python
import functools

import jax
import jax.numpy as jnp
from jax import lax
from jax.experimental import pallas as pl
from jax.experimental.pallas import tpu as pltpu

F32 = jnp.float32
BF16 = jnp.bfloat16

EPS = 1e-6
LOG2_E = 1.4426950408889634
CHUNK = 64
GLA_HEADS = 4
GLA_GATE_NORM = 16.0
SSM_HEADDIM = 64
SSM_GROUPS = 8
SSM_DSTATE = 128
SSM_CONV = 4
ATT_HD = 64
ATT_LEFT_CHUNKS = 8
ATT_MAX_REL = 256

V7X_VMEM_BYTES = 64 * 1024 * 1024
VMEM_LIMIT_BYTES = V7X_VMEM_BYTES - 8 * 1024 * 1024

TOKEN_BLOCK = 512
FFN_TOKEN_BLOCK = 1024
SEQ_BLOCK = 256
GLA_SEQ_BLOCK = 512


def _params(*semantics):
    return pltpu.CompilerParams(dimension_semantics=semantics, vmem_limit_bytes=VMEM_LIMIT_BYTES)


def _resident(shape):
    zeros = (0,) * len(shape)
    return pl.BlockSpec(shape, lambda *_: zeros, pipeline_mode=pl.Buffered(1))


def _dot(a, b):
    return jnp.dot(a, b, preferred_element_type=F32)


def _dot_nt(a, b):
    return lax.dot_general(a, b, (((1,), (1,)), ((), ())), preferred_element_type=F32)


def _dot_tn(a, b):
    return lax.dot_general(a, b, (((0,), (0,)), ((), ())), preferred_element_type=F32)


def _rms(x, w):
    return x * lax.rsqrt(jnp.mean(x * x, axis=-1, keepdims=True) + EPS) * w


def _split3(x):
    hi = x.astype(BF16)
    r = x - hi.astype(F32)
    mid = r.astype(BF16)
    lo = (r - mid.astype(F32)).astype(BF16)
    return hi, mid, lo


def _sel_left(sel, x):
    hi, mid, lo = _split3(x)
    return _dot(sel, hi) + _dot(sel, mid) + _dot(sel, lo)


def _sel_right(x, sel):
    hi, mid, lo = _split3(x)
    return _dot(hi, sel) + _dot(mid, sel) + _dot(lo, sel)


def _tri(n, lower):
    r = lax.broadcasted_iota(jnp.int32, (n, n), 0)
    c = lax.broadcasted_iota(jnp.int32, (n, n), 1)
    return (r >= c) if lower else (r <= c)


def _ffn_kernel(h_ref, nw_ref, wg_ref, wu_ref, wd_ref, fw_ref, o_ref, *, f_chunks, final_norm):
    x = h_ref[...]
    xn = _rms(x, nw_ref[...]).astype(BF16)
    f_dim = wg_ref.shape[1]
    fc = f_dim // f_chunks
    def gate_up(i):
        return _dot(xn, wg_ref[:, i * fc:(i + 1) * fc]), _dot(xn, wu_ref[:, i * fc:(i + 1) * fc])

    acc = None
    nxt = gate_up(0)
    for i in range(f_chunks):
        g, u = nxt
        if i + 1 < f_chunks:
            nxt = gate_up(i + 1)
        a = (g * jax.nn.sigmoid(g) * u).astype(BF16)
        y = _dot(a, wd_ref[i * fc:(i + 1) * fc, :])
        acc = y if acc is None else acc + y
    out = x + 0.5 * acc
    if final_norm:
        out = _rms(out, fw_ref[...])
    o_ref[...] = out


def _ffn(h, norm_w, wg, wu, wd, final_w, final_norm):
    t, d = h.shape
    f_dim = wg.shape[1]
    bm = min(FFN_TOKEN_BLOCK, t)
    row = pl.BlockSpec((bm, d), lambda i: (i, 0))
    return pl.pallas_call(
        functools.partial(_ffn_kernel, f_chunks=11, final_norm=final_norm),
        grid=(t // bm,),
        in_specs=[row, _resident((1, d)), _resident((d, f_dim)), _resident((d, f_dim)),
                  _resident((f_dim, d)), _resident((1, d))],
        out_specs=row,
        out_shape=jax.ShapeDtypeStruct((t, d), F32),
        compiler_params=_params("parallel"),
        name="ffn",
    )(h, norm_w, wg, wu, wd, final_w)


def _proj_gla_kernel(h_ref, nw_ref, w_ref, wc_ref, wgk_ref, bgk_ref, o_ref, gk_ref):
    xn = _rms(h_ref[...], nw_ref[...]).astype(BF16)
    o_ref[...] = _dot(xn, w_ref[...]).astype(BF16)
    code = _dot(xn, wc_ref[...]).astype(BF16)
    pre = _dot(code, wgk_ref[...]) + bgk_ref[...]
    gk_ref[...] = jax.nn.log_sigmoid(pre) * (LOG2_E / GLA_GATE_NORM)


LANES = 128
CONV_HALO = 8


CONV_COL_TILE = 512
CONV_ROW_TILE = 64
CONV_ROW_PIECE = 128


def _proj_ssm_kernel(h_ref, nw_ref, w_ref, wdt_ref, dtb_ref, cw_ref, cb_ref, o_ref, dt_ref, carry_ref, acc_ref,
                     xn_ref, *, d_inner, blocks_per_seq):
    assert SSM_CONV == 4
    bm = h_ref.shape[0]

    @pl.when(pl.program_id(0) % blocks_per_seq == 0)
    def _():
        carry_ref[...] = jnp.zeros_like(carry_ref)

    xn_ref[...] = _rms(h_ref[...], nw_ref[...]).astype(BF16)
    dt_ref[...] = jax.nn.softplus(_dot(xn_ref[...], wdt_ref[...]) + dtb_ref[...])
    conv_dim = w_ref.shape[1] - d_inner
    n_tiles = conv_dim // CONV_COL_TILE

    n_pieces = bm // CONV_ROW_PIECE

    def matmul_piece(k, p):
        rows = slice(p * CONV_ROW_PIECE, (p + 1) * CONV_ROW_PIECE)
        if k < n_tiles:
            c0 = k * CONV_COL_TILE
            if p == 0:
                acc_ref[k % 2, 0:CONV_HALO, :] = carry_ref[:, c0:c0 + CONV_COL_TILE]
            acc_ref[k % 2, CONV_HALO + rows.start:CONV_HALO + rows.stop, :] = _dot(
                xn_ref[rows, :], w_ref[:, d_inner + c0:d_inner + c0 + CONV_COL_TILE])
        else:
            c0 = (k - n_tiles) * CONV_COL_TILE
            o_ref[rows, c0:c0 + CONV_COL_TILE] = _dot(xn_ref[rows, :], w_ref[:, c0:c0 + CONV_COL_TILE]).astype(BF16)

    def conv_piece(k, p):
        c0 = k * CONV_COL_TILE
        for r0 in range(p * CONV_ROW_PIECE, (p + 1) * CONV_ROW_PIECE, CONV_ROW_TILE):
            for l0 in range(0, CONV_COL_TILE, LANES):
                cols = slice(c0 + l0, c0 + l0 + LANES)
                x0 = acc_ref[k % 2, r0:r0 + CONV_HALO + CONV_ROW_TILE, l0:l0 + LANES]
                x1 = pltpu.roll(x0, 1, axis=0)
                near = cw_ref[3:4, cols] * x0 + cw_ref[2:3, cols] * x1
                far = pltpu.roll(cw_ref[1:2, cols] * x0 + cw_ref[0:1, cols] * x1, 2, axis=0)
                conv = (cb_ref[:, cols] + near + far)[CONV_HALO:, :]
                o_ref[r0:r0 + CONV_ROW_TILE, d_inner + c0 + l0:d_inner + c0 + l0 + LANES] = (
                    conv * jax.nn.sigmoid(conv)).astype(BF16)
        if p == n_pieces - 1:
            carry_ref[:, c0:c0 + CONV_COL_TILE] = acc_ref[k % 2, bm:bm + CONV_HALO, :]

    for p in range(n_pieces):
        matmul_piece(0, p)
    for k in range(n_tiles + d_inner // CONV_COL_TILE):
        for p in range(n_pieces):
            if k + 1 < n_tiles + d_inner // CONV_COL_TILE:
                matmul_piece(k + 1, p)
            if k < n_tiles:
                conv_piece(k, p)


def _proj_att_kernel(h_ref, nw_ref, w_ref, o_ref):
    xn = _rms(h_ref[...], nw_ref[...]).astype(BF16)
    o_ref[...] = _dot(xn, w_ref[...]).astype(BF16)


def _proj_gate_kernel(h_ref, nw_ref, w_ref, b_ref, o_ref):
    xn = _rms(h_ref[...], nw_ref[...]).astype(BF16)
    o_ref[...] = jax.nn.sigmoid(_dot(xn, w_ref[...]) + b_ref[...]).astype(BF16)


def _proj_call(kernel, name, h, consts, outs, scratch=(), carries=False):
    t, d = h.shape
    bm = min(TOKEN_BLOCK, t)
    in_specs = [pl.BlockSpec((bm, d), lambda i: (i, 0))] + [_resident(c.shape) for c in consts]
    out_specs = [pl.BlockSpec((bm, cols), lambda i: (i, 0)) for cols, _ in outs]
    out_shape = [jax.ShapeDtypeStruct((t, cols), dtype) for cols, dtype in outs]
    return pl.pallas_call(
        kernel, grid=(t // bm,), in_specs=in_specs, out_specs=out_specs, out_shape=out_shape,
        scratch_shapes=list(scratch),
        compiler_params=_params("arbitrary" if carries else "parallel"), name=name,
    )(h, *consts)


def _gla_kernel(x_ref, gk_ref, nw_ref, o_ref, state_ref, *, dk, dv):
    hk, hv = dk // GLA_HEADS, dv // GLA_HEADS
    scale = hk ** -0.5

    @pl.when(pl.program_id(1) == 0)
    def _():
        state_ref[...] = jnp.zeros_like(state_ref)

    tril = _tri(CHUNK, True)
    tril_b = tril.astype(BF16)
    nw = nw_ref[...]
    for j in range(x_ref.shape[0] // CHUNK):
        rows = slice(j * CHUNK, (j + 1) * CHUNK)
        b_all = _sel_left(tril_b, gk_ref[rows, :])
        for h in range(GLA_HEADS):
            q = x_ref[rows, h * hk:(h + 1) * hk].astype(F32) * scale
            k = x_ref[rows, dk + h * hk:dk + (h + 1) * hk].astype(F32)
            v = x_ref[rows, 2 * dk + h * hv:2 * dk + (h + 1) * hv]
            r = x_ref[rows, 2 * dk + dv + h * hv:2 * dk + dv + (h + 1) * hv].astype(F32)
            b = b_all[:, h * hk:(h + 1) * hk]
            b_last = b[CHUNK - 1:CHUNK, :]
            b_mid = b[CHUNK // 2:CHUNK // 2 + 1, :]
            a = _dot_nt((q * jnp.exp2(b - b_mid)).astype(BF16), (k * jnp.exp2(b_mid - b)).astype(BF16))
            a = jnp.where(tril, a, 0.0)
            st = state_ref[h]
            o = _dot(a.astype(BF16), v) + _dot_nt((q * jnp.exp2(b)).astype(BF16), st.astype(BF16))
            state_ref[h] = jnp.exp2(b_last) * st + _dot_tn(v, (k * jnp.exp2(b_last - b)).astype(BF16))
            o = _rms(o, nw) * (r * jax.nn.sigmoid(r))
            o_ref[rows, h * hv:(h + 1) * hv] = o.astype(BF16)


def _chunk_tril(n):
    pos = jnp.arange(n)
    return ((pos[:, None] >= pos[None, :]) & (pos[:, None] // CHUNK == pos[None, :] // CHUNK)).astype(BF16)


def _gla(x, gk, norm_w, dk, dv):
    bsz, seq, width = x.shape
    lb = min(GLA_SEQ_BLOCK, seq)
    hk, hv = dk // GLA_HEADS, dv // GLA_HEADS
    return pl.pallas_call(
        functools.partial(_gla_kernel, dk=dk, dv=dv),
        grid=(bsz, seq // lb),
        in_specs=[pl.BlockSpec((None, lb, width), lambda b, i: (b, i, 0)),
                  pl.BlockSpec((None, lb, dk), lambda b, i: (b, i, 0)),
                  _resident((1, hv))],
        out_specs=pl.BlockSpec((None, lb, dv), lambda b, i: (b, i, 0)),
        out_shape=jax.ShapeDtypeStruct((bsz, seq, dv), BF16),
        scratch_shapes=[pltpu.VMEM((GLA_HEADS, hv, hk), F32)],
        compiler_params=_params("parallel", "arbitrary"),
        name="gla",
    )(x, gk, norm_w)


def _ssd_kernel(x_ref, dt_ref, alog_ref, dexp_ref, nw_ref, tblk_ref, e3_ref, e2_ref, bd_ref,
                o_ref, state_ref, *, d_inner):
    lb = x_ref.shape[0]
    gn = SSM_GROUPS * SSM_DSTATE
    gw = d_inner // SSM_GROUPS
    hpg = gw // SSM_HEADDIM
    b_off, c_off = 2 * d_inner, 2 * d_inner + gn

    @pl.when(pl.program_id(1) == 0)
    def _():
        state_ref[...] = jnp.zeros_like(state_ref)

    dt = dt_ref[...]
    cum = _sel_left(tblk_ref[...], dt * (-LOG2_E * jnp.exp(alog_ref[...])))
    cum_exp = _dot(jnp.concatenate(_split3(cum), axis=1), e3_ref[...])
    dt_hi, dt_mid, _ = _split3(dt)
    dt_exp = _dot(jnp.concatenate([dt_hi, dt_mid], axis=1), e2_ref[...])

    row = lax.broadcasted_iota(jnp.int32, (CHUNK, d_inner), 0)
    key = lax.rem(lax.broadcasted_iota(jnp.int32, (CHUNK, d_inner), 1), SSM_HEADDIM)
    diag = row == key
    causal = (row >= key)[:, :gw]
    blockdiag = bd_ref[...]
    for j in range(lb // CHUNK):
        rows = slice(j * CHUNK, (j + 1) * CHUNK)
        ce = cum_exp[rows, :]
        cum_s = jnp.sum(jnp.where(diag, ce, 0.0), axis=0, keepdims=True)
        cum_last = ce[CHUNK - 1:CHUNK, :]
        for g in range(SSM_GROUPS):
            cols = slice(g * gw, (g + 1) * gw)
            bg = x_ref[rows, b_off + g * SSM_DSTATE:b_off + (g + 1) * SSM_DSTATE]
            cg = x_ref[rows, c_off + g * SSM_DSTATE:c_off + (g + 1) * SSM_DSTATE]
            cb = _dot_nt(cg, jnp.concatenate([bg, bg], axis=0))
            cb = jnp.concatenate([cb] * (hpg // 2), axis=1)
            ceg = ce[:, cols]
            lam = jnp.exp2(jnp.where(causal, ceg - cum_s[:, cols], -jnp.inf))
            m = (cb * lam).astype(BF16)
            xg = x_ref[rows, d_inner + g * gw:d_inner + (g + 1) * gw].astype(F32)
            xdt = xg * dt_exp[rows, cols]
            xbd = jnp.concatenate([xdt.astype(BF16)] * hpg, axis=0) * blockdiag
            st = state_ref[g]
            y = _dot(m, xbd) + _dot(cg, st.astype(BF16)) * jnp.exp2(ceg) + dexp_ref[:, cols] * xg
            to_end = jnp.exp2(cum_last[:, cols] - ceg)
            state_ref[g] = jnp.exp2(cum_last[:, cols]) * st + _dot_tn(bg, (xdt * to_end).astype(BF16))
            z = x_ref[rows, cols].astype(F32)
            o_ref[rows, cols] = _rms(y * (z * jax.nn.sigmoid(z)), nw_ref[:, cols]).astype(BF16)


def _ssd(x, dt, a_log, d_exp, norm_w, d_inner):
    bsz, seq, width = x.shape
    heads = dt.shape[-1]
    assert CHUNK == SSM_HEADDIM and d_inner == heads * SSM_HEADDIM
    lb = min(SEQ_BLOCK, seq)
    gw = d_inner // SSM_GROUPS
    tblk = _chunk_tril(lb)
    expand = (jnp.arange(heads)[:, None] == jnp.arange(d_inner)[None, :] // SSM_HEADDIM).astype(BF16)
    same_head = (jnp.arange(gw)[:, None] // CHUNK == jnp.arange(gw)[None, :] // SSM_HEADDIM).astype(BF16)
    consts = [a_log, d_exp, norm_w, tblk, jnp.tile(expand, (3, 1)), jnp.tile(expand, (2, 1)), same_head]
    return pl.pallas_call(
        functools.partial(_ssd_kernel, d_inner=d_inner),
        grid=(bsz, seq // lb),
        in_specs=[pl.BlockSpec((None, lb, width), lambda b, i: (b, i, 0)),
                  pl.BlockSpec((None, lb, heads), lambda b, i: (b, i, 0))] + [_resident(c.shape) for c in consts],
        out_specs=pl.BlockSpec((None, lb, d_inner), lambda b, i: (b, i, 0)),
        out_shape=jax.ShapeDtypeStruct((bsz, seq, d_inner), BF16),
        scratch_shapes=[pltpu.VMEM((SSM_GROUPS, SSM_DSTATE, gw), F32)],
        compiler_params=_params("parallel", "arbitrary"),
        name="ssd",
    )(x, dt, *consts)


ATT_Q_CHUNKS = 4
ATT_HEAD_GROUP = 4


def _att_kernel(x_ref, bias_ref, o_ref, *, width):
    seq = x_ref.shape[0]
    pad = ATT_LEFT_CHUNKS * CHUNK
    qrows = ATT_Q_CHUNKS * CHUNK
    win = pad + qrows
    gwid = ATT_HEAD_GROUP * ATT_HD
    scale = ATT_HD ** -0.5
    head_of_lane = lax.broadcasted_iota(jnp.int32, (qrows, gwid), 1) // ATT_HD

    def block(r0, k0, nk):
        for hq in range(width // gwid):
            cols = slice(hq * gwid, (hq + 1) * gwid)
            q = x_ref[pl.ds(r0, qrows), cols] * scale
            qm = jnp.concatenate([jnp.where(head_of_lane == h, q, 0) for h in range(ATT_HEAD_GROUP)], axis=0)
            kb = x_ref[pl.ds(k0, nk), width + hq * gwid:width + (hq + 1) * gwid]
            vb = x_ref[pl.ds(k0, nk), 2 * width + hq * gwid:2 * width + (hq + 1) * gwid]
            s = _dot_nt(qm.astype(BF16), kb)
            probs = []
            for h in range(ATT_HEAD_GROUP):
                sh = s[h * qrows:(h + 1) * qrows, :] + bias_ref[hq * ATT_HEAD_GROUP + h, :, win - nk:win]
                e = jnp.exp(sh - jnp.max(sh, axis=-1, keepdims=True))
                probs.append((e * (1.0 / jnp.sum(e, axis=-1, keepdims=True))).astype(BF16))
            o4 = _dot(jnp.concatenate(probs, axis=0), vb)
            o = o4[(ATT_HEAD_GROUP - 1) * qrows:, :]
            for h in range(ATT_HEAD_GROUP - 2, -1, -1):
                o = jnp.where(head_of_lane == h, o4[h * qrows:(h + 1) * qrows, :], o)
            o_ref[pl.ds(r0, qrows), cols] = o.astype(BF16)

    n_head = pad // qrows
    for i in range(n_head):
        block(i * qrows, 0, (i + 1) * qrows)

    def body(i, carry):
        r0 = pl.multiple_of(i * qrows, qrows)
        block(r0, pl.multiple_of(r0 - pad, qrows), win)
        return carry

    lax.fori_loop(n_head, seq // qrows, body, 0)


def _att(x, bias, width):
    bsz, seq, _ = x.shape
    assert seq % (ATT_Q_CHUNKS * CHUNK) == 0 and (ATT_LEFT_CHUNKS * CHUNK) % (ATT_Q_CHUNKS * CHUNK) == 0
    assert width % (ATT_HEAD_GROUP * ATT_HD) == 0
    return pl.pallas_call(
        functools.partial(_att_kernel, width=width),
        grid=(bsz,),
        in_specs=[pl.BlockSpec((None, seq, 3 * width), lambda b: (b, 0, 0)), _resident(bias.shape)],
        out_specs=pl.BlockSpec((None, seq, width), lambda b: (b, 0, 0)),
        out_shape=jax.ShapeDtypeStruct((bsz, seq, width), BF16),
        compiler_params=_params("parallel"),
        name="att",
    )(x, bias)


def _att_bias(rel_table):
    pad = ATT_LEFT_CHUNKS * CHUNK
    qrows = ATT_Q_CHUNKS * CHUNK
    win = pad + qrows
    n = qrows + win
    m = jnp.arange(n)
    off = jnp.where(m < win, m, m - n)
    u = rel_table[jnp.clip(pad - off, -ATT_MAX_REL, ATT_MAX_REL) + ATT_MAX_REL].astype(F32).T
    toep = jnp.tile(u, (1, qrows))[:, :qrows * (n - 1)].reshape(-1, qrows, n - 1)[:, :, :win]
    qc = jnp.arange(qrows)[:, None] // CHUNK
    kc = jnp.arange(win)[None, :] // CHUNK
    return jnp.where((kc >= qc) & (kc <= qc + ATT_LEFT_CHUNKS), toep, -jnp.inf)


def _merge_kernel(h_ref, a_ref, s_ref, c_ref, g_ref, wa_ref, ws_ref, wc_ref, wo_ref, o_ref):
    d = h_ref.shape[1]
    ya = _dot(a_ref[...], wa_ref[...])
    ys = _dot(s_ref[...], ws_ref[...])
    yc = _dot(c_ref[...], wc_ref[...])
    merged = (g_ref[:, 0:d].astype(F32) * ya + g_ref[:, d:2 * d].astype(F32) * ys
              + g_ref[:, 2 * d:3 * d].astype(F32) * yc)
    o_ref[...] = h_ref[...] + _dot(merged.astype(BF16), wo_ref[...])


def _merge(h, o_gla, o_ssm, o_att, gates, wa, ws, wc, wo):
    t, d = h.shape
    bm = min(TOKEN_BLOCK, t)

    def row(cols):
        return pl.BlockSpec((bm, cols), lambda i: (i, 0))

    return pl.pallas_call(
        _merge_kernel, grid=(t // bm,),
        in_specs=[row(d), row(o_gla.shape[1]), row(o_ssm.shape[1]), row(o_att.shape[1]), row(3 * d),
                  _resident(wa.shape), _resident(ws.shape), _resident(wc.shape), _resident(wo.shape)],
        out_specs=row(d),
        out_shape=jax.ShapeDtypeStruct((t, d), F32),
        compiler_params=_params("parallel"),
        name="merge",
    )(h, o_gla, o_ssm, o_att, gates, wa, ws, wc, wo)


def _mixer_layer(h, bsz, seq, mix_norm, w_in, gla_w_gk, gla_b_gk, gla_norm, ssm_conv_w, ssm_conv_b, ssm_dt_bias,
                 ssm_A_log, ssm_D, ssm_norm, attn_rel_bias, gate_bias, w_branch_gla, w_branch_ssm, w_branch_attn,
                 w_out):
    t, d = h.shape
    rank, gla_dk = gla_w_gk.shape
    gla_dv = w_branch_gla.shape[0]
    d_inner = w_branch_ssm.shape[0]
    conv_dim = ssm_conv_w.shape[1]
    heads = ssm_A_log.shape[0]
    att_w = w_branch_attn.shape[0]
    sizes = (gla_dk, gla_dk, gla_dv, gla_dv, rank, d_inner, conv_dim, heads, att_w, att_w, att_w, 3 * d)
    offs = [0]
    for s in sizes:
        offs.append(offs[-1] + s)
    w = w_in.astype(BF16)
    nw = mix_norm.reshape(1, d)

    qkvr, gk = _proj_call(
        _proj_gla_kernel, "proj_gla", h,
        [nw, w[:, offs[0]:offs[4]], w[:, offs[4]:offs[5]], gla_w_gk.astype(BF16), gla_b_gk.reshape(1, gla_dk)],
        [(offs[4], BF16), (gla_dk, F32)])
    zx, dt = _proj_call(
        functools.partial(_proj_ssm_kernel, d_inner=d_inner, blocks_per_seq=seq // min(TOKEN_BLOCK, seq)),
        "proj_ssm", h,
        [nw, w[:, offs[5]:offs[7]], w[:, offs[7]:offs[8]], ssm_dt_bias.reshape(1, heads),
         ssm_conv_w, ssm_conv_b.reshape(1, conv_dim)],
        [(offs[7] - offs[5], BF16), (heads, F32)],
        scratch=[pltpu.VMEM((CONV_HALO, conv_dim), F32),
                 pltpu.VMEM((2, CONV_HALO + min(TOKEN_BLOCK, seq), CONV_COL_TILE), F32),
                 pltpu.VMEM((min(TOKEN_BLOCK, seq), d), BF16)], carries=True)
    qkv = _proj_call(_proj_att_kernel, "proj_att", h, [nw, w[:, offs[8]:offs[11]]], [(3 * att_w, BF16)])[0]
    gates = _proj_call(_proj_gate_kernel, "proj_gate", h,
                       [nw, w[:, offs[11]:offs[12]], gate_bias.reshape(1, 3 * d)], [(3 * d, BF16)])[0]

    o_gla = _gla(qkvr.reshape(bsz, seq, -1), gk.reshape(bsz, seq, gla_dk),
                 gla_norm.reshape(1, -1), gla_dk, gla_dv)
    o_ssm = _ssd(zx.reshape(bsz, seq, -1), dt.reshape(bsz, seq, heads), ssm_A_log.reshape(1, heads),
                 jnp.repeat(ssm_D, SSM_HEADDIM).reshape(1, d_inner), ssm_norm.reshape(1, d_inner), d_inner)
    o_att = _att(qkv.reshape(bsz, seq, -1), _att_bias(attn_rel_bias), att_w)
    return _merge(h, o_gla.reshape(t, -1), o_ssm.reshape(t, -1), o_att.reshape(t, -1), gates,
                  w_branch_gla.astype(BF16), w_branch_ssm.astype(BF16), w_branch_attn.astype(BF16),
                  w_out.astype(BF16))


def kernel(x, ffn1_norm, ffn1_w_gate, ffn1_w_up, ffn1_w_down, mix_norm, w_in, gla_w_gk, gla_b_gk, gla_norm, ssm_conv_w, ssm_conv_b, ssm_dt_bias, ssm_A_log, ssm_D, ssm_norm, attn_rel_bias, gate_bias, w_branch_gla, w_branch_ssm, w_branch_attn, w_out, ffn2_norm, ffn2_w_gate, ffn2_w_up, ffn2_w_down, final_norm):
    bsz, seq, d = x.shape
    depth = w_in.shape[0]
    h = x.reshape(bsz * seq, d)
    fw = final_norm.reshape(1, d)
    for l in range(depth):
        h = _ffn(h, ffn1_norm[l].reshape(1, d), ffn1_w_gate[l].astype(BF16), ffn1_w_up[l].astype(BF16),
                 ffn1_w_down[l].astype(BF16), fw, False)
        h = _mixer_layer(h, bsz, seq, mix_norm[l], w_in[l], gla_w_gk[l], gla_b_gk[l], gla_norm[l], ssm_conv_w[l],
                         ssm_conv_b[l], ssm_dt_bias[l], ssm_A_log[l], ssm_D[l], ssm_norm[l], attn_rel_bias[l],
                         gate_bias[l], w_branch_gla[l], w_branch_ssm[l], w_branch_attn[l], w_out[l])
        h = _ffn(h, ffn2_norm[l].reshape(1, d), ffn2_w_gate[l].astype(BF16), ffn2_w_up[l].astype(BF16),
                 ffn2_w_down[l].astype(BF16), fw, l == depth - 1)
    return h.reshape(bsz, seq, d)
```

```python
import functools

import jax
import jax.numpy as jnp
from jax import lax
from jax.experimental import pallas as pl
from jax.experimental.pallas import tpu as pltpu

F32 = jnp.float32
BF16 = jnp.bfloat16

EPS = 1e-6
LOG2_E = 1.4426950408889634
CHUNK = 64
GLA_HEADS = 4
GLA_GATE_NORM = 16.0
SSM_HEADDIM = 64
SSM_GROUPS = 8
SSM_DSTATE = 128
SSM_CONV = 4
ATT_HD = 64
ATT_LEFT_CHUNKS = 8
ATT_MAX_REL = 256

V7X_VMEM_BYTES = 64 * 1024 * 1024
VMEM_LIMIT_BYTES = V7X_VMEM_BYTES - 8 * 1024 * 1024

TOKEN_BLOCK = 512
FFN_TOKEN_BLOCK = 1024
SEQ_BLOCK = 256
GLA_SEQ_BLOCK = 512


def _params(*semantics):
    return pltpu.CompilerParams(dimension_semantics=semantics, vmem_limit_bytes=VMEM_LIMIT_BYTES)


def _resident(shape):
    zeros = (0,) * len(shape)
    return pl.BlockSpec(shape, lambda *_: zeros, pipeline_mode=pl.Buffered(1))


def _dot(a, b):
    return jnp.dot(a, b, preferred_element_type=F32)


def _dot_nt(a, b):
    return lax.dot_general(a, b, (((1,), (1,)), ((), ())), preferred_element_type=F32)


def _dot_tn(a, b):
    return lax.dot_general(a, b, (((0,), (0,)), ((), ())), preferred_element_type=F32)


def _rms(x, w):
    return x * lax.rsqrt(jnp.mean(x * x, axis=-1, keepdims=True) + EPS) * w


def _split3(x):
    hi = x.astype(BF16)
    r = x - hi.astype(F32)
    mid = r.astype(BF16)
    lo = (r - mid.astype(F32)).astype(BF16)
    return hi, mid, lo


def _sel_left(sel, x):
    hi, mid, lo = _split3(x)
    return _dot(sel, hi) + _dot(sel, mid) + _dot(sel, lo)


def _sel_right(x, sel):
    hi, mid, lo = _split3(x)
    return _dot(hi, sel) + _dot(mid, sel) + _dot(lo, sel)


def _tri(n, lower):
    r = lax.broadcasted_iota(jnp.int32, (n, n), 0)
    c = lax.broadcasted_iota(jnp.int32, (n, n), 1)
    return (r >= c) if lower else (r <= c)


def _ffn_kernel(h_ref, nw_ref, wg_ref, wu_ref, wd_ref, fw_ref, o_ref, *, f_chunks, final_norm):
    x = h_ref[...]
    xn = _rms(x, nw_ref[...]).astype(BF16)
    f_dim = wg_ref.shape[1]
    fc = f_dim // f_chunks
    def gate_up(i):
        return _dot(xn, wg_ref[:, i * fc:(i + 1) * fc]), _dot(xn, wu_ref[:, i * fc:(i + 1) * fc])

    acc = None
    nxt = gate_up(0)
    for i in range(f_chunks):
        g, u = nxt
        if i + 1 < f_chunks:
            nxt = gate_up(i + 1)
        a = (g * jax.nn.sigmoid(g) * u).astype(BF16)
        y = _dot(a, wd_ref[i * fc:(i + 1) * fc, :])
        acc = y if acc is None else acc + y
    out = x + 0.5 * acc
    if final_norm:
        out = _rms(out, fw_ref[...])
    o_ref[...] = out


def _ffn(h, norm_w, wg, wu, wd, final_w, final_norm):
    t, d = h.shape
    f_dim = wg.shape[1]
    bm = min(FFN_TOKEN_BLOCK, t)
    row = pl.BlockSpec((bm, d), lambda i: (i, 0))
    return pl.pallas_call(
        functools.partial(_ffn_kernel, f_chunks=11, final_norm=final_norm),
        grid=(t // bm,),
        in_specs=[row, _resident((1, d)), _resident((d, f_dim)), _resident((d, f_dim)),
                  _resident((f_dim, d)), _resident((1, d))],
        out_specs=row,
        out_shape=jax.ShapeDtypeStruct((t, d), F32),
        compiler_params=_params("parallel"),
        name="ffn",
    )(h, norm_w, wg, wu, wd, final_w)


def _proj_gla_kernel(h_ref, nw_ref, w_ref, wc_ref, wgk_ref, bgk_ref, o_ref, gk_ref):
    xn = _rms(h_ref[...], nw_ref[...]).astype(BF16)
    o_ref[...] = _dot(xn, w_ref[...]).astype(BF16)
    code = _dot(xn, wc_ref[...]).astype(BF16)
    pre = _dot(code, wgk_ref[...]) + bgk_ref[...]
    gk_ref[...] = jax.nn.log_sigmoid(pre) * (LOG2_E / GLA_GATE_NORM)


LANES = 128
CONV_HALO = 8


CONV_COL_TILE = 512
CONV_ROW_TILE = 64
CONV_ROW_PIECE = TOKEN_BLOCK


def _proj_ssm_kernel(h_ref, nw_ref, w_ref, wdt_ref, dtb_ref, cw_ref, cb_ref, o_ref, dt_ref, carry_ref, acc_ref,
                     xn_ref, *, d_inner, blocks_per_seq):
    assert SSM_CONV == 4
    bm = h_ref.shape[0]

    @pl.when(pl.program_id(0) % blocks_per_seq == 0)
    def _():
        carry_ref[...] = jnp.zeros_like(carry_ref)

    xn_ref[...] = _rms(h_ref[...], nw_ref[...]).astype(BF16)
    dt_ref[...] = jax.nn.softplus(_dot(xn_ref[...], wdt_ref[...]) + dtb_ref[...])
    conv_dim = w_ref.shape[1] - d_inner
    n_tiles = conv_dim // CONV_COL_TILE

    piece = min(CONV_ROW_PIECE, bm)
    n_pieces = bm // piece

    def matmul_piece(k, p):
        rows = slice(p * piece, (p + 1) * piece)
        if k < n_tiles:
            c0 = k * CONV_COL_TILE
            if p == 0:
                acc_ref[k % 2, 0:CONV_HALO, :] = carry_ref[:, c0:c0 + CONV_COL_TILE]
            acc_ref[k % 2, CONV_HALO + rows.start:CONV_HALO + rows.stop, :] = _dot(
                xn_ref[rows, :], w_ref[:, d_inner + c0:d_inner + c0 + CONV_COL_TILE])
        else:
            c0 = (k - n_tiles) * CONV_COL_TILE
            o_ref[rows, c0:c0 + CONV_COL_TILE] = _dot(xn_ref[rows, :], w_ref[:, c0:c0 + CONV_COL_TILE]).astype(BF16)

    def conv_piece(k, p):
        c0 = k * CONV_COL_TILE
        for r0 in range(p * piece, (p + 1) * piece, CONV_ROW_TILE):
            for l0 in range(0, CONV_COL_TILE, LANES):
                cols = slice(c0 + l0, c0 + l0 + LANES)
                x0 = acc_ref[k % 2, r0:r0 + CONV_HALO + CONV_ROW_TILE, l0:l0 + LANES]
                x1 = pltpu.roll(x0, 1, axis=0)
                near = cw_ref[3:4, cols] * x0 + cw_ref[2:3, cols] * x1
                far = pltpu.roll(cw_ref[1:2, cols] * x0 + cw_ref[0:1, cols] * x1, 2, axis=0)
                conv = (cb_ref[:, cols] + near + far)[CONV_HALO:, :]
                o_ref[r0:r0 + CONV_ROW_TILE, d_inner + c0 + l0:d_inner + c0 + l0 + LANES] = (
                    conv * jax.nn.sigmoid(conv)).astype(BF16)
        if p == n_pieces - 1:
            carry_ref[:, c0:c0 + CONV_COL_TILE] = acc_ref[k % 2, bm:bm + CONV_HALO, :]

    for p in range(n_pieces):
        matmul_piece(0, p)
    for k in range(n_tiles + d_inner // CONV_COL_TILE):
        for p in range(n_pieces):
            if k + 1 < n_tiles + d_inner // CONV_COL_TILE:
                matmul_piece(k + 1, p)
            if k < n_tiles:
                conv_piece(k, p)


def _proj_att_kernel(h_ref, nw_ref, w_ref, o_ref):
    xn = _rms(h_ref[...], nw_ref[...]).astype(BF16)
    o_ref[...] = _dot(xn, w_ref[...]).astype(BF16)


def _proj_gate_kernel(h_ref, nw_ref, w_ref, b_ref, o_ref):
    xn = _rms(h_ref[...], nw_ref[...]).astype(BF16)
    o_ref[...] = jax.nn.sigmoid(_dot(xn, w_ref[...]) + b_ref[...]).astype(BF16)


def _proj_call(kernel, name, h, consts, outs, scratch=(), carries=False):
    t, d = h.shape
    bm = min(TOKEN_BLOCK, t)
    in_specs = [pl.BlockSpec((bm, d), lambda i: (i, 0))] + [_resident(c.shape) for c in consts]
    out_specs = [pl.BlockSpec((bm, cols), lambda i: (i, 0)) for cols, _ in outs]
    out_shape = [jax.ShapeDtypeStruct((t, cols), dtype) for cols, dtype in outs]
    return pl.pallas_call(
        kernel, grid=(t // bm,), in_specs=in_specs, out_specs=out_specs, out_shape=out_shape,
        scratch_shapes=list(scratch),
        compiler_params=_params("arbitrary" if carries else "parallel"), name=name,
    )(h, *consts)


def _gla_kernel(x_ref, gk_ref, nw_ref, o_ref, state_ref, *, dk, dv):
    hk, hv = dk // GLA_HEADS, dv // GLA_HEADS
    scale = hk ** -0.5

    @pl.when(pl.program_id(1) == 0)
    def _():
        state_ref[...] = jnp.zeros_like(state_ref)

    tril = _tri(CHUNK, True)
    tril_b = tril.astype(BF16)
    nw = nw_ref[...]
    for j in range(x_ref.shape[0] // CHUNK):
        rows = slice(j * CHUNK, (j + 1) * CHUNK)
        b_all = _sel_left(tril_b, gk_ref[rows, :])
        for h in range(GLA_HEADS):
            q = x_ref[rows, h * hk:(h + 1) * hk].astype(F32) * scale
            k = x_ref[rows, dk + h * hk:dk + (h + 1) * hk].astype(F32)
            v = x_ref[rows, 2 * dk + h * hv:2 * dk + (h + 1) * hv]
            r = x_ref[rows, 2 * dk + dv + h * hv:2 * dk + dv + (h + 1) * hv].astype(F32)
            b = b_all[:, h * hk:(h + 1) * hk]
            b_last = b[CHUNK - 1:CHUNK, :]
            b_mid = b[CHUNK // 2:CHUNK // 2 + 1, :]
            a = _dot_nt((q * jnp.exp2(b - b_mid)).astype(BF16), (k * jnp.exp2(b_mid - b)).astype(BF16))
            a = jnp.where(tril, a, 0.0)
            st = state_ref[h]
            o = _dot(a.astype(BF16), v) + _dot_nt((q * jnp.exp2(b)).astype(BF16), st.astype(BF16))
            state_ref[h] = jnp.exp2(b_last) * st + _dot_tn(v, (k * jnp.exp2(b_last - b)).astype(BF16))
            o = _rms(o, nw) * (r * jax.nn.sigmoid(r))
            o_ref[rows, h * hv:(h + 1) * hv] = o.astype(BF16)


def _chunk_tril(n):
    pos = jnp.arange(n)
    return ((pos[:, None] >= pos[None, :]) & (pos[:, None] // CHUNK == pos[None, :] // CHUNK)).astype(BF16)


def _gla(x, gk, norm_w, dk, dv):
    bsz, seq, width = x.shape
    lb = min(GLA_SEQ_BLOCK, seq)
    hk, hv = dk // GLA_HEADS, dv // GLA_HEADS
    return pl.pallas_call(
        functools.partial(_gla_kernel, dk=dk, dv=dv),
        grid=(bsz, seq // lb),
        in_specs=[pl.BlockSpec((None, lb, width), lambda b, i: (b, i, 0)),
                  pl.BlockSpec((None, lb, dk), lambda b, i: (b, i, 0)),
                  _resident((1, hv))],
        out_specs=pl.BlockSpec((None, lb, dv), lambda b, i: (b, i, 0)),
        out_shape=jax.ShapeDtypeStruct((bsz, seq, dv), BF16),
        scratch_shapes=[pltpu.VMEM((GLA_HEADS, hv, hk), F32)],
        compiler_params=_params("parallel", "arbitrary"),
        name="gla",
    )(x, gk, norm_w)


def _ssd_kernel(x_ref, dt_ref, alog_ref, dexp_ref, nw_ref, tblk_ref, e3_ref, e2_ref, bd_ref,
                o_ref, state_ref, *, d_inner):
    lb = x_ref.shape[0]
    gn = SSM_GROUPS * SSM_DSTATE
    gw = d_inner // SSM_GROUPS
    hpg = gw // SSM_HEADDIM
    b_off, c_off = 2 * d_inner, 2 * d_inner + gn

    @pl.when(pl.program_id(1) == 0)
    def _():
        state_ref[...] = jnp.zeros_like(state_ref)

    dt = dt_ref[...]
    cum = _sel_left(tblk_ref[...], dt * (-LOG2_E * jnp.exp(alog_ref[...])))
    cum_exp = _dot(jnp.concatenate(_split3(cum), axis=1), e3_ref[...])
    dt_hi, dt_mid, _ = _split3(dt)
    dt_exp = _dot(jnp.concatenate([dt_hi, dt_mid], axis=1), e2_ref[...])

    row = lax.broadcasted_iota(jnp.int32, (CHUNK, d_inner), 0)
    key = lax.rem(lax.broadcasted_iota(jnp.int32, (CHUNK, d_inner), 1), SSM_HEADDIM)
    diag = row == key
    causal = (row >= key)[:, :gw]
    blockdiag = bd_ref[...]
    for j in range(lb // CHUNK):
        rows = slice(j * CHUNK, (j + 1) * CHUNK)
        ce = cum_exp[rows, :]
        cum_s = jnp.sum(jnp.where(diag, ce, 0.0), axis=0, keepdims=True)
        cum_last = ce[CHUNK - 1:CHUNK, :]
        for g in range(SSM_GROUPS):
            cols = slice(g * gw, (g + 1) * gw)
            bg = x_ref[rows, b_off + g * SSM_DSTATE:b_off + (g + 1) * SSM_DSTATE]
            cg = x_ref[rows, c_off + g * SSM_DSTATE:c_off + (g + 1) * SSM_DSTATE]
            cb = _dot_nt(cg, jnp.concatenate([bg, bg], axis=0))
            cb = jnp.concatenate([cb] * (hpg // 2), axis=1)
            ceg = ce[:, cols]
            lam = jnp.exp2(jnp.where(causal, ceg - cum_s[:, cols], -jnp.inf))
            m = (cb * lam).astype(BF16)
            xg = x_ref[rows, d_inner + g * gw:d_inner + (g + 1) * gw].astype(F32)
            xdt = xg * dt_exp[rows, cols]
            xbd = jnp.concatenate([xdt.astype(BF16)] * hpg, axis=0) * blockdiag
            st = state_ref[g]
            y = _dot(m, xbd) + _dot(cg, st.astype(BF16)) * jnp.exp2(ceg) + dexp_ref[:, cols] * xg
            to_end = jnp.exp2(cum_last[:, cols] - ceg)
            state_ref[g] = jnp.exp2(cum_last[:, cols]) * st + _dot_tn(bg, (xdt * to_end).astype(BF16))
            z = x_ref[rows, cols].astype(F32)
            o_ref[rows, cols] = _rms(y * (z * jax.nn.sigmoid(z)), nw_ref[:, cols]).astype(BF16)


def _ssd(x, dt, a_log, d_exp, norm_w, d_inner):
    bsz, seq, width = x.shape
    heads = dt.shape[-1]
    assert CHUNK == SSM_HEADDIM and d_inner == heads * SSM_HEADDIM
    lb = min(SEQ_BLOCK, seq)
    gw = d_inner // SSM_GROUPS
    tblk = _chunk_tril(lb)
    expand = (jnp.arange(heads)[:, None] == jnp.arange(d_inner)[None, :] // SSM_HEADDIM).astype(BF16)
    same_head = (jnp.arange(gw)[:, None] // CHUNK == jnp.arange(gw)[None, :] // SSM_HEADDIM).astype(BF16)
    consts = [a_log, d_exp, norm_w, tblk, jnp.tile(expand, (3, 1)), jnp.tile(expand, (2, 1)), same_head]
    return pl.pallas_call(
        functools.partial(_ssd_kernel, d_inner=d_inner),
        grid=(bsz, seq // lb),
        in_specs=[pl.BlockSpec((None, lb, width), lambda b, i: (b, i, 0)),
                  pl.BlockSpec((None, lb, heads), lambda b, i: (b, i, 0))] + [_resident(c.shape) for c in consts],
        out_specs=pl.BlockSpec((None, lb, d_inner), lambda b, i: (b, i, 0)),
        out_shape=jax.ShapeDtypeStruct((bsz, seq, d_inner), BF16),
        scratch_shapes=[pltpu.VMEM((SSM_GROUPS, SSM_DSTATE, gw), F32)],
        compiler_params=_params("parallel", "arbitrary"),
        name="ssd",
    )(x, dt, *consts)


ATT_Q_CHUNKS = 4
ATT_HEAD_GROUP = 4


def _att_kernel(x_ref, bias_ref, o_ref, *, width):
    seq = x_ref.shape[0]
    pad = ATT_LEFT_CHUNKS * CHUNK
    qrows = ATT_Q_CHUNKS * CHUNK
    win = pad + qrows
    gwid = ATT_HEAD_GROUP * ATT_HD
    scale = ATT_HD ** -0.5
    head_of_lane = lax.broadcasted_iota(jnp.int32, (qrows, gwid), 1) // ATT_HD

    def block(r0, k0, nk):
        n_heads = width // ATT_HD

        def scores(head):
            hq, h = divmod(head, ATT_HEAD_GROUP)
            q = x_ref[pl.ds(r0, qrows), hq * gwid:(hq + 1) * gwid] * scale
            kb = x_ref[pl.ds(k0, nk), width + hq * gwid:width + (hq + 1) * gwid]
            return _dot_nt(jnp.where(head_of_lane == h, q, 0).astype(BF16), kb)

        nxt = scores(0)
        o = None
        for head in range(n_heads):
            hq, h = divmod(head, ATT_HEAD_GROUP)
            s = nxt
            if head + 1 < n_heads:
                nxt = scores(head + 1)
            sh = s + bias_ref[head, :, win - nk:win]
            e = jnp.exp(sh - jnp.max(sh, axis=-1, keepdims=True))
            prob = (e * (1.0 / jnp.sum(e, axis=-1, keepdims=True))).astype(BF16)
            vb = x_ref[pl.ds(k0, nk), 2 * width + hq * gwid:2 * width + (hq + 1) * gwid]
            oh = _dot(prob, vb)
            o = oh if h == 0 else jnp.where(head_of_lane == h, oh, o)
            if h == ATT_HEAD_GROUP - 1:
                o_ref[pl.ds(r0, qrows), hq * gwid:(hq + 1) * gwid] = o.astype(BF16)

    n_head = pad // qrows
    for i in range(n_head):
        block(i * qrows, 0, (i + 1) * qrows)

    def body(i, carry):
        r0 = pl.multiple_of(i * qrows, qrows)
        block(r0, pl.multiple_of(r0 - pad, qrows), win)
        return carry

    lax.fori_loop(n_head, seq // qrows, body, 0)


def _att(x, bias, width):
    bsz, seq, _ = x.shape
    assert seq % (ATT_Q_CHUNKS * CHUNK) == 0 and (ATT_LEFT_CHUNKS * CHUNK) % (ATT_Q_CHUNKS * CHUNK) == 0
    assert width % (ATT_HEAD_GROUP * ATT_HD) == 0
    return pl.pallas_call(
        functools.partial(_att_kernel, width=width),
        grid=(bsz,),
        in_specs=[pl.BlockSpec((None, seq, 3 * width), lambda b: (b, 0, 0)), _resident(bias.shape)],
        out_specs=pl.BlockSpec((None, seq, width), lambda b: (b, 0, 0)),
        out_shape=jax.ShapeDtypeStruct((bsz, seq, width), BF16),
        compiler_params=_params("parallel"),
        name="att",
    )(x, bias)


def _att_bias(rel_table):
    pad = ATT_LEFT_CHUNKS * CHUNK
    qrows = ATT_Q_CHUNKS * CHUNK
    win = pad + qrows
    n = qrows + win
    m = jnp.arange(n)
    off = jnp.where(m < win, m, m - n)
    u = rel_table[jnp.clip(pad - off, -ATT_MAX_REL, ATT_MAX_REL) + ATT_MAX_REL].astype(F32).T
    toep = jnp.tile(u, (1, qrows))[:, :qrows * (n - 1)].reshape(-1, qrows, n - 1)[:, :, :win]
    qc = jnp.arange(qrows)[:, None] // CHUNK
    kc = jnp.arange(win)[None, :] // CHUNK
    return jnp.where((kc >= qc) & (kc <= qc + ATT_LEFT_CHUNKS), toep, -jnp.inf)


def _merge_kernel(h_ref, a_ref, s_ref, c_ref, g_ref, wa_ref, ws_ref, wc_ref, wo_ref, o_ref):
    d = h_ref.shape[1]
    ya = _dot(a_ref[...], wa_ref[...])
    ys = _dot(s_ref[...], ws_ref[...])
    yc = _dot(c_ref[...], wc_ref[...])
    merged = (g_ref[:, 0:d].astype(F32) * ya + g_ref[:, d:2 * d].astype(F32) * ys
              + g_ref[:, 2 * d:3 * d].astype(F32) * yc)
    o_ref[...] = h_ref[...] + _dot(merged.astype(BF16), wo_ref[...])


def _merge(h, o_gla, o_ssm, o_att, gates, wa, ws, wc, wo):
    t, d = h.shape
    bm = min(TOKEN_BLOCK, t)

    def row(cols):
        return pl.BlockSpec((bm, cols), lambda i: (i, 0))

    return pl.pallas_call(
        _merge_kernel, grid=(t // bm,),
        in_specs=[row(d), row(o_gla.shape[1]), row(o_ssm.shape[1]), row(o_att.shape[1]), row(3 * d),
                  _resident(wa.shape), _resident(ws.shape), _resident(wc.shape), _resident(wo.shape)],
        out_specs=row(d),
        out_shape=jax.ShapeDtypeStruct((t, d), F32),
        compiler_params=_params("parallel"),
        name="merge",
    )(h, o_gla, o_ssm, o_att, gates, wa, ws, wc, wo)


def _mixer_layer(h, bsz, seq, mix_norm, w_in, gla_w_gk, gla_b_gk, gla_norm, ssm_conv_w, ssm_conv_b, ssm_dt_bias,
                 ssm_A_log, ssm_D, ssm_norm, attn_rel_bias, gate_bias, w_branch_gla, w_branch_ssm, w_branch_attn,
                 w_out):
    t, d = h.shape
    rank, gla_dk = gla_w_gk.shape
    gla_dv = w_branch_gla.shape[0]
    d_inner = w_branch_ssm.shape[0]
    conv_dim = ssm_conv_w.shape[1]
    heads = ssm_A_log.shape[0]
    att_w = w_branch_attn.shape[0]
    sizes = (gla_dk, gla_dk, gla_dv, gla_dv, rank, d_inner, conv_dim, heads, att_w, att_w, att_w, 3 * d)
    offs = [0]
    for s in sizes:
        offs.append(offs[-1] + s)
    w = w_in.astype(BF16)
    nw = mix_norm.reshape(1, d)

    qkvr, gk = _proj_call(
        _proj_gla_kernel, "proj_gla", h,
        [nw, w[:, offs[0]:offs[4]], w[:, offs[4]:offs[5]], gla_w_gk.astype(BF16), gla_b_gk.reshape(1, gla_dk)],
        [(offs[4], BF16), (gla_dk, F32)])
    zx, dt = _proj_call(
        functools.partial(_proj_ssm_kernel, d_inner=d_inner, blocks_per_seq=seq // min(TOKEN_BLOCK, seq)),
        "proj_ssm", h,
        [nw, w[:, offs[5]:offs[7]], w[:, offs[7]:offs[8]], ssm_dt_bias.reshape(1, heads),
         ssm_conv_w, ssm_conv_b.reshape(1, conv_dim)],
        [(offs[7] - offs[5], BF16), (heads, F32)],
        scratch=[pltpu.VMEM((CONV_HALO, conv_dim), F32),
                 pltpu.VMEM((2, CONV_HALO + min(TOKEN_BLOCK, seq), CONV_COL_TILE), F32),
                 pltpu.VMEM((min(TOKEN_BLOCK, seq), d), BF16)], carries=True)
    qkv = _proj_call(_proj_att_kernel, "proj_att", h, [nw, w[:, offs[8]:offs[11]]], [(3 * att_w, BF16)])[0]
    gates = _proj_call(_proj_gate_kernel, "proj_gate", h,
                       [nw, w[:, offs[11]:offs[12]], gate_bias.reshape(1, 3 * d)], [(3 * d, BF16)])[0]

    o_gla = _gla(qkvr.reshape(bsz, seq, -1), gk.reshape(bsz, seq, gla_dk),
                 gla_norm.reshape(1, -1), gla_dk, gla_dv)
    o_ssm = _ssd(zx.reshape(bsz, seq, -1), dt.reshape(bsz, seq, heads), ssm_A_log.reshape(1, heads),
                 jnp.repeat(ssm_D, SSM_HEADDIM).reshape(1, d_inner), ssm_norm.reshape(1, d_inner), d_inner)
    o_att = _att(qkv.reshape(bsz, seq, -1), _att_bias(attn_rel_bias), att_w)
    return _merge(h, o_gla.reshape(t, -1), o_ssm.reshape(t, -1), o_att.reshape(t, -1), gates,
                  w_branch_gla.astype(BF16), w_branch_ssm.astype(BF16), w_branch_attn.astype(BF16),
                  w_out.astype(BF16))


def kernel(x, ffn1_norm, ffn1_w_gate, ffn1_w_up, ffn1_w_down, mix_norm, w_in, gla_w_gk, gla_b_gk, gla_norm, ssm_conv_w, ssm_conv_b, ssm_dt_bias, ssm_A_log, ssm_D, ssm_norm, attn_rel_bias, gate_bias, w_branch_gla, w_branch_ssm, w_branch_attn, w_out, ffn2_norm, ffn2_w_gate, ffn2_w_up, ffn2_w_down, final_norm):
    bsz, seq, d = x.shape
    depth = w_in.shape[0]
    h = x.reshape(bsz * seq, d)
    fw = final_norm.reshape(1, d)
    for l in range(depth):
        h = _ffn(h, ffn1_norm[l].reshape(1, d), ffn1_w_gate[l].astype(BF16), ffn1_w_up[l].astype(BF16),
                 ffn1_w_down[l].astype(BF16), fw, False)
        h = _mixer_layer(h, bsz, seq, mix_norm[l], w_in[l], gla_w_gk[l], gla_b_gk[l], gla_norm[l], ssm_conv_w[l],
                         ssm_conv_b[l], ssm_dt_bias[l], ssm_A_log[l], ssm_D[l], ssm_norm[l], attn_rel_bias[l],
                         gate_bias[l], w_branch_gla[l], w_branch_ssm[l], w_branch_attn[l], w_out[l])
        h = _ffn(h, ffn2_norm[l].reshape(1, d), ffn2_w_gate[l].astype(BF16), ffn2_w_up[l].astype(BF16),
                 ffn2_w_down[l].astype(BF16), fw, l == depth - 1)
    return h.reshape(bsz, seq, d)
```

```python
import functools

import jax
import jax.numpy as jnp
from jax import lax
from jax.experimental import pallas as pl
from jax.experimental.pallas import tpu as pltpu

F32 = jnp.float32
BF16 = jnp.bfloat16

EPS = 1e-6
LOG2_E = 1.4426950408889634
CHUNK = 64
GLA_HEADS = 4
GLA_GATE_NORM = 16.0
SSM_HEADDIM = 64
SSM_GROUPS = 8
SSM_DSTATE = 128
SSM_CONV = 4
ATT_HD = 64
ATT_LEFT_CHUNKS = 8
ATT_MAX_REL = 256

V7X_VMEM_BYTES = 64 * 1024 * 1024
VMEM_LIMIT_BYTES = V7X_VMEM_BYTES - 8 * 1024 * 1024

TOKEN_BLOCK = 512
FFN_TOKEN_BLOCK = 1024
SEQ_BLOCK = 512
GLA_SEQ_BLOCK = 512


def _params(*semantics):
    return pltpu.CompilerParams(dimension_semantics=semantics, vmem_limit_bytes=VMEM_LIMIT_BYTES)


def _resident(shape):
    zeros = (0,) * len(shape)
    return pl.BlockSpec(shape, lambda *_: zeros, pipeline_mode=pl.Buffered(1))


def _dot(a, b):
    return jnp.dot(a, b, preferred_element_type=F32)


def _dot_nt(a, b):
    return lax.dot_general(a, b, (((1,), (1,)), ((), ())), preferred_element_type=F32)


def _dot_tn(a, b):
    return lax.dot_general(a, b, (((0,), (0,)), ((), ())), preferred_element_type=F32)


def _rms(x, w):
    return x * lax.rsqrt(jnp.mean(x * x, axis=-1, keepdims=True) + EPS) * w


def _split3(x):
    hi = x.astype(BF16)
    r = x - hi.astype(F32)
    mid = r.astype(BF16)
    lo = (r - mid.astype(F32)).astype(BF16)
    return hi, mid, lo


def _sel_left(sel, x):
    hi, mid, lo = _split3(x)
    return _dot(sel, hi) + _dot(sel, mid) + _dot(sel, lo)


def _sel_right(x, sel):
    hi, mid, lo = _split3(x)
    return _dot(hi, sel) + _dot(mid, sel) + _dot(lo, sel)


def _tri(n, lower):
    r = lax.broadcasted_iota(jnp.int32, (n, n), 0)
    c = lax.broadcasted_iota(jnp.int32, (n, n), 1)
    return (r >= c) if lower else (r <= c)


def _ffn_kernel(h_ref, nw_ref, wg_ref, wu_ref, wd_ref, fw_ref, o_ref, *, f_chunks, final_norm):
    x = h_ref[...]
    xn = _rms(x, nw_ref[...]).astype(BF16)
    f_dim = wg_ref.shape[1]
    fc = f_dim // f_chunks
    def gate_up(i):
        return _dot(xn, wg_ref[:, i * fc:(i + 1) * fc]), _dot(xn, wu_ref[:, i * fc:(i + 1) * fc])

    acc = None
    nxt = gate_up(0)
    for i in range(f_chunks):
        g, u = nxt
        if i + 1 < f_chunks:
            nxt = gate_up(i + 1)
        a = (g * jax.nn.sigmoid(g) * u).astype(BF16)
        y = _dot(a, wd_ref[i * fc:(i + 1) * fc, :])
        acc = y if acc is None else acc + y
    out = x + 0.5 * acc
    if final_norm:
        out = _rms(out, fw_ref[...])
    o_ref[...] = out


def _ffn(h, norm_w, wg, wu, wd, final_w, final_norm):
    t, d = h.shape
    f_dim = wg.shape[1]
    bm = min(FFN_TOKEN_BLOCK, t)
    row = pl.BlockSpec((bm, d), lambda i: (i, 0))
    return pl.pallas_call(
        functools.partial(_ffn_kernel, f_chunks=11, final_norm=final_norm),
        grid=(t // bm,),
        in_specs=[row, _resident((1, d)), _resident((d, f_dim)), _resident((d, f_dim)),
                  _resident((f_dim, d)), _resident((1, d))],
        out_specs=row,
        out_shape=jax.ShapeDtypeStruct((t, d), F32),
        compiler_params=_params("parallel"),
        name="ffn",
    )(h, norm_w, wg, wu, wd, final_w)


def _proj_main_kernel(h_ref, nw_ref, wgla_ref, wc_ref, wgk_ref, bgk_ref, watt_ref, wgate_ref, bgate_ref,
                      gla_ref, gk_ref, att_ref, gate_ref):
    xn = _rms(h_ref[...], nw_ref[...]).astype(BF16)
    gla_ref[...] = _dot(xn, wgla_ref[...]).astype(BF16)
    code = _dot(xn, wc_ref[...]).astype(BF16)
    pre = _dot(code, wgk_ref[...]) + bgk_ref[...]
    gk_ref[...] = jax.nn.log_sigmoid(pre) * (LOG2_E / GLA_GATE_NORM)
    att_ref[...] = _dot(xn, watt_ref[...]).astype(BF16)
    gate_ref[...] = jax.nn.sigmoid(_dot(xn, wgate_ref[...]) + bgate_ref[...]).astype(BF16)


LANES = 128
CONV_HALO = 8


CONV_COL_TILE = 512
CONV_ROW_TILE = 64
CONV_ROW_PIECE = TOKEN_BLOCK


def _proj_ssm_kernel(h_ref, nw_ref, w_ref, wdt_ref, dtb_ref, cw_ref, cb_ref, o_ref, dt_ref, carry_ref, acc_ref,
                     xn_ref, *, d_inner, blocks_per_seq):
    assert SSM_CONV == 4
    bm = h_ref.shape[0]

    @pl.when(pl.program_id(0) % blocks_per_seq == 0)
    def _():
        carry_ref[...] = jnp.zeros_like(carry_ref)

    xn_ref[...] = _rms(h_ref[...], nw_ref[...]).astype(BF16)
    dt_ref[...] = jax.nn.softplus(_dot(xn_ref[...], wdt_ref[...]) + dtb_ref[...])
    conv_dim = w_ref.shape[1] - d_inner
    n_tiles = conv_dim // CONV_COL_TILE

    piece = min(CONV_ROW_PIECE, bm)
    n_pieces = bm // piece

    def matmul_piece(k, p):
        rows = slice(p * piece, (p + 1) * piece)
        if k < n_tiles:
            c0 = k * CONV_COL_TILE
            if p == 0:
                acc_ref[k % 2, 0:CONV_HALO, :] = carry_ref[:, c0:c0 + CONV_COL_TILE]
            acc_ref[k % 2, CONV_HALO + rows.start:CONV_HALO + rows.stop, :] = _dot(
                xn_ref[rows, :], w_ref[:, d_inner + c0:d_inner + c0 + CONV_COL_TILE])
        else:
            c0 = (k - n_tiles) * CONV_COL_TILE
            o_ref[rows, c0:c0 + CONV_COL_TILE] = _dot(xn_ref[rows, :], w_ref[:, c0:c0 + CONV_COL_TILE]).astype(BF16)

    def conv_piece(k, p):
        c0 = k * CONV_COL_TILE
        for r0 in range(p * piece, (p + 1) * piece, CONV_ROW_TILE):
            for l0 in range(0, CONV_COL_TILE, LANES):
                cols = slice(c0 + l0, c0 + l0 + LANES)
                x0 = acc_ref[k % 2, r0:r0 + CONV_HALO + CONV_ROW_TILE, l0:l0 + LANES]
                x1 = pltpu.roll(x0, 1, axis=0)
                near = cw_ref[3:4, cols] * x0 + cw_ref[2:3, cols] * x1
                far = pltpu.roll(cw_ref[1:2, cols] * x0 + cw_ref[0:1, cols] * x1, 2, axis=0)
                conv = (cb_ref[:, cols] + near + far)[CONV_HALO:, :]
                o_ref[r0:r0 + CONV_ROW_TILE, d_inner + c0 + l0:d_inner + c0 + l0 + LANES] = (
                    conv * jax.nn.sigmoid(conv)).astype(BF16)
        if p == n_pieces - 1:
            carry_ref[:, c0:c0 + CONV_COL_TILE] = acc_ref[k % 2, bm:bm + CONV_HALO, :]

    for p in range(n_pieces):
        matmul_piece(0, p)
    for k in range(n_tiles + d_inner // CONV_COL_TILE):
        for p in range(n_pieces):
            if k + 1 < n_tiles + d_inner // CONV_COL_TILE:
                matmul_piece(k + 1, p)
            if k < n_tiles:
                conv_piece(k, p)


def _proj_call(kernel, name, h, consts, outs, scratch=(), carries=False):
    t, d = h.shape
    bm = min(TOKEN_BLOCK, t)
    in_specs = [pl.BlockSpec((bm, d), lambda i: (i, 0))] + [_resident(c.shape) for c in consts]
    out_specs = [pl.BlockSpec((bm, cols), lambda i: (i, 0)) for cols, _ in outs]
    out_shape = [jax.ShapeDtypeStruct((t, cols), dtype) for cols, dtype in outs]
    return pl.pallas_call(
        kernel, grid=(t // bm,), in_specs=in_specs, out_specs=out_specs, out_shape=out_shape,
        scratch_shapes=list(scratch),
        compiler_params=_params("arbitrary" if carries else "parallel"), name=name,
    )(h, *consts)


def _gla_kernel(x_ref, gk_ref, nw_ref, o_ref, state_ref, *, dk, dv):
    hk, hv = dk // GLA_HEADS, dv // GLA_HEADS
    scale = hk ** -0.5

    @pl.when(pl.program_id(1) == 0)
    def _():
        state_ref[...] = jnp.zeros_like(state_ref)

    tril = _tri(CHUNK, True)
    tril_b = tril.astype(BF16)
    nw = nw_ref[...]
    for j in range(x_ref.shape[0] // CHUNK):
        rows = slice(j * CHUNK, (j + 1) * CHUNK)
        b_all = _sel_left(tril_b, gk_ref[rows, :])
        for h in range(GLA_HEADS):
            q = x_ref[rows, h * hk:(h + 1) * hk].astype(F32) * scale
            k = x_ref[rows, dk + h * hk:dk + (h + 1) * hk].astype(F32)
            v = x_ref[rows, 2 * dk + h * hv:2 * dk + (h + 1) * hv]
            r = x_ref[rows, 2 * dk + dv + h * hv:2 * dk + dv + (h + 1) * hv].astype(F32)
            b = b_all[:, h * hk:(h + 1) * hk]
            b_last = b[CHUNK - 1:CHUNK, :]
            b_mid = b[CHUNK // 2:CHUNK // 2 + 1, :]
            a = _dot_nt((q * jnp.exp2(b - b_mid)).astype(BF16), (k * jnp.exp2(b_mid - b)).astype(BF16))
            a = jnp.where(tril, a, 0.0)
            st = state_ref[h]
            o = _dot(a.astype(BF16), v) + _dot_nt((q * jnp.exp2(b)).astype(BF16), st.astype(BF16))
            state_ref[h] = jnp.exp2(b_last) * st + _dot_tn(v, (k * jnp.exp2(b_last - b)).astype(BF16))
            o = _rms(o, nw) * (r * jax.nn.sigmoid(r))
            o_ref[rows, h * hv:(h + 1) * hv] = o.astype(BF16)


def _chunk_tril(n):
    pos = jnp.arange(n)
    return ((pos[:, None] >= pos[None, :]) & (pos[:, None] // CHUNK == pos[None, :] // CHUNK)).astype(BF16)


def _gla(x, gk, norm_w, dk, dv):
    bsz, seq, width = x.shape
    lb = min(GLA_SEQ_BLOCK, seq)
    hk, hv = dk // GLA_HEADS, dv // GLA_HEADS
    return pl.pallas_call(
        functools.partial(_gla_kernel, dk=dk, dv=dv),
        grid=(bsz, seq // lb),
        in_specs=[pl.BlockSpec((None, lb, width), lambda b, i: (b, i, 0)),
                  pl.BlockSpec((None, lb, dk), lambda b, i: (b, i, 0)),
                  _resident((1, hv))],
        out_specs=pl.BlockSpec((None, lb, dv), lambda b, i: (b, i, 0)),
        out_shape=jax.ShapeDtypeStruct((bsz, seq, dv), BF16),
        scratch_shapes=[pltpu.VMEM((GLA_HEADS, hv, hk), F32)],
        compiler_params=_params("parallel", "arbitrary"),
        name="gla",
    )(x, gk, norm_w)


def _ssd_kernel(x_ref, dt_ref, alog_ref, dexp_ref, nw_ref, tblk_ref, e3_ref, e2_ref, bd_ref,
                o_ref, state_ref, *, d_inner):
    lb = x_ref.shape[0]
    gn = SSM_GROUPS * SSM_DSTATE
    gw = d_inner // SSM_GROUPS
    hpg = gw // SSM_HEADDIM
    b_off, c_off = 2 * d_inner, 2 * d_inner + gn

    @pl.when(pl.program_id(1) == 0)
    def _():
        state_ref[...] = jnp.zeros_like(state_ref)

    dt = dt_ref[...]
    cum = _sel_left(tblk_ref[...], dt * (-LOG2_E * jnp.exp(alog_ref[...])))
    cum_exp = _dot(jnp.concatenate(_split3(cum), axis=1), e3_ref[...])
    dt_hi, dt_mid, _ = _split3(dt)
    dt_exp = _dot(jnp.concatenate([dt_hi, dt_mid], axis=1), e2_ref[...])

    row = lax.broadcasted_iota(jnp.int32, (CHUNK, d_inner), 0)
    key = lax.rem(lax.broadcasted_iota(jnp.int32, (CHUNK, d_inner), 1), SSM_HEADDIM)
    diag = row == key
    causal = (row >= key)[:, :gw]
    blockdiag = bd_ref[...]
    for j in range(lb // CHUNK):
        rows = slice(j * CHUNK, (j + 1) * CHUNK)
        ce = cum_exp[rows, :]
        cum_s = jnp.sum(jnp.where(diag, ce, 0.0), axis=0, keepdims=True)
        cum_last = ce[CHUNK - 1:CHUNK, :]
        for g in range(SSM_GROUPS):
            cols = slice(g * gw, (g + 1) * gw)
            bg = x_ref[rows, b_off + g * SSM_DSTATE:b_off + (g + 1) * SSM_DSTATE]
            cg = x_ref[rows, c_off + g * SSM_DSTATE:c_off + (g + 1) * SSM_DSTATE]
            cb = _dot_nt(cg, jnp.concatenate([bg, bg], axis=0))
            cb = jnp.concatenate([cb] * (hpg // 2), axis=1)
            ceg = ce[:, cols]
            lam = jnp.exp2(jnp.where(causal, ceg - cum_s[:, cols], -jnp.inf))
            m = (cb * lam).astype(BF16)
            xg = x_ref[rows, d_inner + g * gw:d_inner + (g + 1) * gw].astype(F32)
            xdt = xg * dt_exp[rows, cols]
            xbd = jnp.concatenate([xdt.astype(BF16)] * hpg, axis=0) * blockdiag
            st = state_ref[g]
            y = _dot(m, xbd) + _dot(cg, st.astype(BF16)) * jnp.exp2(ceg) + dexp_ref[:, cols] * xg
            to_end = jnp.exp2(cum_last[:, cols] - ceg)
            state_ref[g] = jnp.exp2(cum_last[:, cols]) * st + _dot_tn(bg, (xdt * to_end).astype(BF16))
            z = x_ref[rows, cols].astype(F32)
            o_ref[rows, cols] = _rms(y * (z * jax.nn.sigmoid(z)), nw_ref[:, cols]).astype(BF16)


def _ssd(x, dt, a_log, d_exp, norm_w, d_inner):
    bsz, seq, width = x.shape
    heads = dt.shape[-1]
    assert CHUNK == SSM_HEADDIM and d_inner == heads * SSM_HEADDIM
    lb = min(SEQ_BLOCK, seq)
    gw = d_inner // SSM_GROUPS
    tblk = _chunk_tril(lb)
    expand = (jnp.arange(heads)[:, None] == jnp.arange(d_inner)[None, :] // SSM_HEADDIM).astype(BF16)
    same_head = (jnp.arange(gw)[:, None] // CHUNK == jnp.arange(gw)[None, :] // SSM_HEADDIM).astype(BF16)
    consts = [a_log, d_exp, norm_w, tblk, jnp.tile(expand, (3, 1)), jnp.tile(expand, (2, 1)), same_head]
    return pl.pallas_call(
        functools.partial(_ssd_kernel, d_inner=d_inner),
        grid=(bsz, seq // lb),
        in_specs=[pl.BlockSpec((None, lb, width), lambda b, i: (b, i, 0)),
                  pl.BlockSpec((None, lb, heads), lambda b, i: (b, i, 0))] + [_resident(c.shape) for c in consts],
        out_specs=pl.BlockSpec((None, lb, d_inner), lambda b, i: (b, i, 0)),
        out_shape=jax.ShapeDtypeStruct((bsz, seq, d_inner), BF16),
        scratch_shapes=[pltpu.VMEM((SSM_GROUPS, SSM_DSTATE, gw), F32)],
        compiler_params=_params("parallel", "arbitrary"),
        name="ssd",
    )(x, dt, *consts)


ATT_Q_CHUNKS = 4
ATT_HEAD_GROUP = 4


def _att_kernel(x_ref, bias_ref, o_ref, *, width):
    seq = x_ref.shape[0]
    pad = ATT_LEFT_CHUNKS * CHUNK
    qrows = ATT_Q_CHUNKS * CHUNK
    win = pad + qrows
    gwid = ATT_HEAD_GROUP * ATT_HD
    scale = ATT_HD ** -0.5
    head_of_lane = lax.broadcasted_iota(jnp.int32, (qrows, gwid), 1) // ATT_HD

    def block(r0, k0, nk):
        n_heads = width // ATT_HD

        def scores(head):
            hq, h = divmod(head, ATT_HEAD_GROUP)
            q = x_ref[pl.ds(r0, qrows), hq * gwid:(hq + 1) * gwid] * scale
            kb = x_ref[pl.ds(k0, nk), width + hq * gwid:width + (hq + 1) * gwid]
            return _dot_nt(jnp.where(head_of_lane == h, q, 0).astype(BF16), kb)

        nxt = scores(0)
        o = None
        for head in range(n_heads):
            hq, h = divmod(head, ATT_HEAD_GROUP)
            s = nxt
            if head + 1 < n_heads:
                nxt = scores(head + 1)
            sh = s + bias_ref[head, :, win - nk:win]
            e = jnp.exp(sh - jnp.max(sh, axis=-1, keepdims=True))
            prob = (e * (1.0 / jnp.sum(e, axis=-1, keepdims=True))).astype(BF16)
            vb = x_ref[pl.ds(k0, nk), 2 * width + hq * gwid:2 * width + (hq + 1) * gwid]
            oh = _dot(prob, vb)
            o = oh if h == 0 else jnp.where(head_of_lane == h, oh, o)
            if h == ATT_HEAD_GROUP - 1:
                o_ref[pl.ds(r0, qrows), hq * gwid:(hq + 1) * gwid] = o.astype(BF16)

    n_head = pad // qrows
    for i in range(n_head):
        block(i * qrows, 0, (i + 1) * qrows)

    def body(i, carry):
        r0 = pl.multiple_of(i * qrows, qrows)
        block(r0, pl.multiple_of(r0 - pad, qrows), win)
        return carry

    lax.fori_loop(n_head, seq // qrows, body, 0)


def _att(x, bias, width):
    bsz, seq, _ = x.shape
    assert seq % (ATT_Q_CHUNKS * CHUNK) == 0 and (ATT_LEFT_CHUNKS * CHUNK) % (ATT_Q_CHUNKS * CHUNK) == 0
    assert width % (ATT_HEAD_GROUP * ATT_HD) == 0
    return pl.pallas_call(
        functools.partial(_att_kernel, width=width),
        grid=(bsz,),
        in_specs=[pl.BlockSpec((None, seq, 3 * width), lambda b: (b, 0, 0)), _resident(bias.shape)],
        out_specs=pl.BlockSpec((None, seq, width), lambda b: (b, 0, 0)),
        out_shape=jax.ShapeDtypeStruct((bsz, seq, width), BF16),
        compiler_params=_params("parallel"),
        name="att",
    )(x, bias)


def _att_bias(rel_table):
    pad = ATT_LEFT_CHUNKS * CHUNK
    qrows = ATT_Q_CHUNKS * CHUNK
    win = pad + qrows
    n = qrows + win
    m = jnp.arange(n)
    off = jnp.where(m < win, m, m - n)
    u = rel_table[jnp.clip(pad - off, -ATT_MAX_REL, ATT_MAX_REL) + ATT_MAX_REL].astype(F32).T
    toep = jnp.tile(u, (1, qrows))[:, :qrows * (n - 1)].reshape(-1, qrows, n - 1)[:, :, :win]
    qc = jnp.arange(qrows)[:, None] // CHUNK
    kc = jnp.arange(win)[None, :] // CHUNK
    return jnp.where((kc >= qc) & (kc <= qc + ATT_LEFT_CHUNKS), toep, -jnp.inf)


def _merge_kernel(h_ref, a_ref, s_ref, c_ref, g_ref, wa_ref, ws_ref, wc_ref, wo_ref, o_ref):
    d = h_ref.shape[1]
    ya = _dot(a_ref[...], wa_ref[...])
    ys = _dot(s_ref[...], ws_ref[...])
    yc = _dot(c_ref[...], wc_ref[...])
    merged = (g_ref[:, 0:d].astype(F32) * ya + g_ref[:, d:2 * d].astype(F32) * ys
              + g_ref[:, 2 * d:3 * d].astype(F32) * yc)
    o_ref[...] = h_ref[...] + _dot(merged.astype(BF16), wo_ref[...])


def _merge(h, o_gla, o_ssm, o_att, gates, wa, ws, wc, wo):
    t, d = h.shape
    bm = min(TOKEN_BLOCK, t)

    def row(cols):
        return pl.BlockSpec((bm, cols), lambda i: (i, 0))

    return pl.pallas_call(
        _merge_kernel, grid=(t // bm,),
        in_specs=[row(d), row(o_gla.shape[1]), row(o_ssm.shape[1]), row(o_att.shape[1]), row(3 * d),
                  _resident(wa.shape), _resident(ws.shape), _resident(wc.shape), _resident(wo.shape)],
        out_specs=row(d),
        out_shape=jax.ShapeDtypeStruct((t, d), F32),
        compiler_params=_params("parallel"),
        name="merge",
    )(h, o_gla, o_ssm, o_att, gates, wa, ws, wc, wo)


def _mixer_layer(h, bsz, seq, mix_norm, w_in, gla_w_gk, gla_b_gk, gla_norm, ssm_conv_w, ssm_conv_b, ssm_dt_bias,
                 ssm_A_log, ssm_D, ssm_norm, attn_rel_bias, gate_bias, w_branch_gla, w_branch_ssm, w_branch_attn,
                 w_out):
    t, d = h.shape
    rank, gla_dk = gla_w_gk.shape
    gla_dv = w_branch_gla.shape[0]
    d_inner = w_branch_ssm.shape[0]
    conv_dim = ssm_conv_w.shape[1]
    heads = ssm_A_log.shape[0]
    att_w = w_branch_attn.shape[0]
    sizes = (gla_dk, gla_dk, gla_dv, gla_dv, rank, d_inner, conv_dim, heads, att_w, att_w, att_w, 3 * d)
    offs = [0]
    for s in sizes:
        offs.append(offs[-1] + s)
    w = w_in.astype(BF16)
    nw = mix_norm.reshape(1, d)

    qkvr, gk, qkv, gates = _proj_call(
        _proj_main_kernel, "proj_main", h,
        [nw, w[:, offs[0]:offs[4]], w[:, offs[4]:offs[5]], gla_w_gk.astype(BF16), gla_b_gk.reshape(1, gla_dk),
         w[:, offs[8]:offs[11]], w[:, offs[11]:offs[12]], gate_bias.reshape(1, 3 * d)],
        [(offs[4], BF16), (gla_dk, F32), (3 * att_w, BF16), (3 * d, BF16)])
    zx, dt = _proj_call(
        functools.partial(_proj_ssm_kernel, d_inner=d_inner, blocks_per_seq=seq // min(TOKEN_BLOCK, seq)),
        "proj_ssm", h,
        [nw, w[:, offs[5]:offs[7]], w[:, offs[7]:offs[8]], ssm_dt_bias.reshape(1, heads),
         ssm_conv_w, ssm_conv_b.reshape(1, conv_dim)],
        [(offs[7] - offs[5], BF16), (heads, F32)],
        scratch=[pltpu.VMEM((CONV_HALO, conv_dim), F32),
                 pltpu.VMEM((2, CONV_HALO + min(TOKEN_BLOCK, seq), CONV_COL_TILE), F32),
                 pltpu.VMEM((min(TOKEN_BLOCK, seq), d), BF16)], carries=True)
    o_gla = _gla(qkvr.reshape(bsz, seq, -1), gk.reshape(bsz, seq, gla_dk),
                 gla_norm.reshape(1, -1), gla_dk, gla_dv)
    o_ssm = _ssd(zx.reshape(bsz, seq, -1), dt.reshape(bsz, seq, heads), ssm_A_log.reshape(1, heads),
                 jnp.repeat(ssm_D, SSM_HEADDIM).reshape(1, d_inner), ssm_norm.reshape(1, d_inner), d_inner)
    o_att = _att(qkv.reshape(bsz, seq, -1), _att_bias(attn_rel_bias), att_w)
    return _merge(h, o_gla.reshape(t, -1), o_ssm.reshape(t, -1), o_att.reshape(t, -1), gates,
                  w_branch_gla.astype(BF16), w_branch_ssm.astype(BF16), w_branch_attn.astype(BF16),
                  w_out.astype(BF16))


def kernel(x, ffn1_norm, ffn1_w_gate, ffn1_w_up, ffn1_w_down, mix_norm, w_in, gla_w_gk, gla_b_gk, gla_norm, ssm_conv_w, ssm_conv_b, ssm_dt_bias, ssm_A_log, ssm_D, ssm_norm, attn_rel_bias, gate_bias, w_branch_gla, w_branch_ssm, w_branch_attn, w_out, ffn2_norm, ffn2_w_gate, ffn2_w_up, ffn2_w_down, final_norm):
    bsz, seq, d = x.shape
    depth = w_in.shape[0]
    h = x.reshape(bsz * seq, d)
    fw = final_norm.reshape(1, d)
    for l in range(depth):
        h = _ffn(h, ffn1_norm[l].reshape(1, d), ffn1_w_gate[l].astype(BF16), ffn1_w_up[l].astype(BF16),
                 ffn1_w_down[l].astype(BF16), fw, False)
        h = _mixer_layer(h, bsz, seq, mix_norm[l], w_in[l], gla_w_gk[l], gla_b_gk[l], gla_norm[l], ssm_conv_w[l],
                         ssm_conv_b[l], ssm_dt_bias[l], ssm_A_log[l], ssm_D[l], ssm_norm[l], attn_rel_bias[l],
                         gate_bias[l], w_branch_gla[l], w_branch_ssm[l], w_branch_attn[l], w_out[l])
        h = _ffn(h, ffn2_norm[l].reshape(1, d), ffn2_w_gate[l].astype(BF16), ffn2_w_up[l].astype(BF16),
                 ffn2_w_down[l].astype(BF16), fw, l == depth - 1)
    return h.reshape(bsz, seq, d)
```

```python
import functools

import jax
import jax.numpy as jnp
from jax import lax
from jax.experimental import pallas as pl
from jax.experimental.pallas import tpu as pltpu

F32 = jnp.float32
BF16 = jnp.bfloat16

EPS = 1e-6
LOG2_E = 1.4426950408889634
CHUNK = 64
GLA_HEADS = 4
GLA_GATE_NORM = 16.0
SSM_HEADDIM = 64
SSM_GROUPS = 8
SSM_DSTATE = 128
SSM_CONV = 4
ATT_HD = 64
ATT_LEFT_CHUNKS = 8
ATT_MAX_REL = 256

V7X_VMEM_BYTES = 64 * 1024 * 1024
VMEM_LIMIT_BYTES = V7X_VMEM_BYTES - 8 * 1024 * 1024
LANES = 128
MXU_TILE = 256

TOKEN_BLOCK = 512
FFN_TOKEN_BLOCK = 1024
SEQ_BLOCK = 512
GLA_SEQ_BLOCK = 512


def _params(*semantics):
    return pltpu.CompilerParams(dimension_semantics=semantics, vmem_limit_bytes=VMEM_LIMIT_BYTES)


def _resident(shape):
    zeros = (0,) * len(shape)
    return pl.BlockSpec(shape, lambda *_: zeros, pipeline_mode=pl.Buffered(1))


def _dot(a, b):
    return jnp.dot(a, b, preferred_element_type=F32)


def _dot_nt(a, b):
    return lax.dot_general(a, b, (((1,), (1,)), ((), ())), preferred_element_type=F32)


def _dot_tn(a, b):
    return lax.dot_general(a, b, (((0,), (0,)), ((), ())), preferred_element_type=F32)


def _rms(x, w):
    return x * lax.rsqrt(jnp.mean(x * x, axis=-1, keepdims=True) + EPS) * w


def _split3(x):
    hi = x.astype(BF16)
    r = x - hi.astype(F32)
    mid = r.astype(BF16)
    lo = (r - mid.astype(F32)).astype(BF16)
    return hi, mid, lo


def _sel_left(sel, x):
    hi, mid, lo = _split3(x)
    return _dot(sel, hi) + _dot(sel, mid) + _dot(sel, lo)


def _tril(n):
    return lax.broadcasted_iota(jnp.int32, (n, n), 0) >= lax.broadcasted_iota(jnp.int32, (n, n), 1)


def _ffn_kernel(h_ref, nw_ref, wg_ref, wu_ref, wd_ref, fw_ref, o_ref, *, final_norm):
    x = h_ref[...]
    xn = _rms(x, nw_ref[...]).astype(BF16)
    fc = MXU_TILE
    f_chunks = wg_ref.shape[1] // fc

    def gate_up(i):
        return _dot(xn, wg_ref[:, i * fc:(i + 1) * fc]), _dot(xn, wu_ref[:, i * fc:(i + 1) * fc])

    acc = None
    nxt = gate_up(0)
    for i in range(f_chunks):
        g, u = nxt
        if i + 1 < f_chunks:
            nxt = gate_up(i + 1)
        a = (g * jax.nn.sigmoid(g) * u).astype(BF16)
        y = _dot(a, wd_ref[i * fc:(i + 1) * fc, :])
        acc = y if acc is None else acc + y
    out = x + 0.5 * acc
    if final_norm:
        out = _rms(out, fw_ref[...])
    o_ref[...] = out


def _ffn(h, norm_w, wg, wu, wd, final_w, final_norm):
    t, d = h.shape
    f_dim = wg.shape[1]
    assert f_dim % MXU_TILE == 0
    bm = min(FFN_TOKEN_BLOCK, t)
    row = pl.BlockSpec((bm, d), lambda i: (i, 0))
    return pl.pallas_call(
        functools.partial(_ffn_kernel, final_norm=final_norm),
        grid=(t // bm,),
        in_specs=[row, _resident((1, d)), _resident((d, f_dim)), _resident((d, f_dim)),
                  _resident((f_dim, d)), _resident((1, d))],
        out_specs=row,
        out_shape=jax.ShapeDtypeStruct((t, d), F32),
        compiler_params=_params("parallel"),
        name="ffn",
    )(h, norm_w, wg, wu, wd, final_w)


def _proj_main_kernel(h_ref, nw_ref, wgla_ref, wc_ref, wgk_ref, bgk_ref, watt_ref, wgate_ref, bgate_ref,
                      gla_ref, gk_ref, att_ref, gate_ref):
    xn = _rms(h_ref[...], nw_ref[...]).astype(BF16)
    gla_ref[...] = _dot(xn, wgla_ref[...]).astype(BF16)
    code = _dot(xn, wc_ref[...]).astype(BF16)
    pre = _dot(code, wgk_ref[...]) + bgk_ref[...]
    gk_ref[...] = jax.nn.log_sigmoid(pre) * (LOG2_E / GLA_GATE_NORM)
    att_ref[...] = _dot(xn, watt_ref[...]).astype(BF16)
    gate_ref[...] = jax.nn.sigmoid(_dot(xn, wgate_ref[...]) + bgate_ref[...]).astype(BF16)


CONV_HALO = 8
CONV_COL_TILE = 512
CONV_ROW_TILE = 64


def _proj_ssm_kernel(h_ref, nw_ref, w_ref, wdt_ref, dtb_ref, cw_ref, cb_ref, o_ref, dt_ref, carry_ref, acc_ref,
                     xn_ref, *, d_inner, blocks_per_seq):
    assert SSM_CONV == 4
    bm = h_ref.shape[0]

    @pl.when(pl.program_id(0) % blocks_per_seq == 0)
    def _():
        carry_ref[...] = jnp.zeros_like(carry_ref)

    xn_ref[...] = _rms(h_ref[...], nw_ref[...]).astype(BF16)
    dt_ref[...] = jax.nn.softplus(_dot(xn_ref[...], wdt_ref[...]) + dtb_ref[...])
    n_conv = (w_ref.shape[1] - d_inner) // CONV_COL_TILE
    n_tiles = n_conv + d_inner // CONV_COL_TILE

    def matmul_tile(k):
        if k < n_conv:
            c0 = k * CONV_COL_TILE
            acc_ref[k % 2, 0:CONV_HALO, :] = carry_ref[:, c0:c0 + CONV_COL_TILE]
            acc_ref[k % 2, CONV_HALO:, :] = _dot(xn_ref[...], w_ref[:, d_inner + c0:d_inner + c0 + CONV_COL_TILE])
        else:
            c0 = (k - n_conv) * CONV_COL_TILE
            o_ref[:, c0:c0 + CONV_COL_TILE] = _dot(xn_ref[...], w_ref[:, c0:c0 + CONV_COL_TILE]).astype(BF16)

    def conv_tile(k):
        c0 = k * CONV_COL_TILE
        for r0 in range(0, bm, CONV_ROW_TILE):
            for l0 in range(0, CONV_COL_TILE, LANES):
                cols = slice(c0 + l0, c0 + l0 + LANES)
                x0 = acc_ref[k % 2, r0:r0 + CONV_HALO + CONV_ROW_TILE, l0:l0 + LANES]
                x1 = pltpu.roll(x0, 1, axis=0)
                near = cw_ref[3:4, cols] * x0 + cw_ref[2:3, cols] * x1
                far = pltpu.roll(cw_ref[1:2, cols] * x0 + cw_ref[0:1, cols] * x1, 2, axis=0)
                conv = (cb_ref[:, cols] + near + far)[CONV_HALO:, :]
                o_ref[r0:r0 + CONV_ROW_TILE, d_inner + c0 + l0:d_inner + c0 + l0 + LANES] = (
                    conv * jax.nn.sigmoid(conv)).astype(BF16)
        carry_ref[:, c0:c0 + CONV_COL_TILE] = acc_ref[k % 2, bm:bm + CONV_HALO, :]

    matmul_tile(0)
    for k in range(n_tiles):
        if k + 1 < n_tiles:
            matmul_tile(k + 1)
        if k < n_conv:
            conv_tile(k)


def _proj_call(kernel, name, h, consts, outs, scratch=(), carries=False):
    t, d = h.shape
    bm = min(TOKEN_BLOCK, t)
    in_specs = [pl.BlockSpec((bm, d), lambda i: (i, 0))] + [_resident(c.shape) for c in consts]
    out_specs = [pl.BlockSpec((bm, cols), lambda i: (i, 0)) for cols, _ in outs]
    out_shape = [jax.ShapeDtypeStruct((t, cols), dtype) for cols, dtype in outs]
    return pl.pallas_call(
        kernel, grid=(t // bm,), in_specs=in_specs, out_specs=out_specs, out_shape=out_shape,
        scratch_shapes=list(scratch),
        compiler_params=_params("arbitrary" if carries else "parallel"), name=name,
    )(h, *consts)


def _gla_kernel(x_ref, gk_ref, nw_ref, o_ref, state_ref, *, dk, dv):
    hk, hv = dk // GLA_HEADS, dv // GLA_HEADS
    scale = hk ** -0.5

    @pl.when(pl.program_id(1) == 0)
    def _():
        state_ref[...] = jnp.zeros_like(state_ref)

    tril = _tril(CHUNK)
    tril_b = tril.astype(BF16)
    nw = nw_ref[...]
    for j in range(x_ref.shape[0] // CHUNK):
        rows = slice(j * CHUNK, (j + 1) * CHUNK)
        b_all = _sel_left(tril_b, gk_ref[rows, :])
        for h in range(GLA_HEADS):
            q = x_ref[rows, h * hk:(h + 1) * hk].astype(F32) * scale
            k = x_ref[rows, dk + h * hk:dk + (h + 1) * hk].astype(F32)
            v = x_ref[rows, 2 * dk + h * hv:2 * dk + (h + 1) * hv]
            r = x_ref[rows, 2 * dk + dv + h * hv:2 * dk + dv + (h + 1) * hv].astype(F32)
            b = b_all[:, h * hk:(h + 1) * hk]
            b_last = b[CHUNK - 1:CHUNK, :]
            b_mid = b[CHUNK // 2:CHUNK // 2 + 1, :]
            a = _dot_nt((q * jnp.exp2(b - b_mid)).astype(BF16), (k * jnp.exp2(b_mid - b)).astype(BF16))
            a = jnp.where(tril, a, 0.0)
            st = state_ref[h]
            o = _dot(a.astype(BF16), v) + _dot_nt((q * jnp.exp2(b)).astype(BF16), st.astype(BF16))
            state_ref[h] = jnp.exp2(b_last) * st + _dot_tn(v, (k * jnp.exp2(b_last - b)).astype(BF16))
            o = _rms(o, nw) * (r * jax.nn.sigmoid(r))
            o_ref[rows, h * hv:(h + 1) * hv] = o.astype(BF16)


def _chunk_tril(n):
    pos = jnp.arange(n)
    return ((pos[:, None] >= pos[None, :]) & (pos[:, None] // CHUNK == pos[None, :] // CHUNK)).astype(BF16)


def _gla(x, gk, norm_w, dk, dv):
    bsz, seq, width = x.shape
    lb = min(GLA_SEQ_BLOCK, seq)
    hk, hv = dk // GLA_HEADS, dv // GLA_HEADS
    return pl.pallas_call(
        functools.partial(_gla_kernel, dk=dk, dv=dv),
        grid=(bsz, seq // lb),
        in_specs=[pl.BlockSpec((None, lb, width), lambda b, i: (b, i, 0)),
                  pl.BlockSpec((None, lb, dk), lambda b, i: (b, i, 0)),
                  _resident((1, hv))],
        out_specs=pl.BlockSpec((None, lb, dv), lambda b, i: (b, i, 0)),
        out_shape=jax.ShapeDtypeStruct((bsz, seq, dv), BF16),
        scratch_shapes=[pltpu.VMEM((GLA_HEADS, hv, hk), F32)],
        compiler_params=_params("parallel", "arbitrary"),
        name="gla",
    )(x, gk, norm_w)


def _ssd_kernel(x_ref, dt_ref, alog_ref, dexp_ref, nw_ref, tblk_ref, e3_ref, e2_ref, bd_ref,
                o_ref, state_ref, *, d_inner):
    lb = x_ref.shape[0]
    gn = SSM_GROUPS * SSM_DSTATE
    gw = d_inner // SSM_GROUPS
    hpg = gw // SSM_HEADDIM
    b_off, c_off = 2 * d_inner, 2 * d_inner + gn

    @pl.when(pl.program_id(1) == 0)
    def _():
        state_ref[...] = jnp.zeros_like(state_ref)

    dt = dt_ref[...]
    cum = _sel_left(tblk_ref[...], dt * (-LOG2_E * jnp.exp(alog_ref[...])))
    cum_exp = _dot(jnp.concatenate(_split3(cum), axis=1), e3_ref[...])
    dt_hi, dt_mid, _ = _split3(dt)
    dt_exp = _dot(jnp.concatenate([dt_hi, dt_mid], axis=1), e2_ref[...])

    row = lax.broadcasted_iota(jnp.int32, (CHUNK, d_inner), 0)
    key = lax.rem(lax.broadcasted_iota(jnp.int32, (CHUNK, d_inner), 1), SSM_HEADDIM)
    diag = row == key
    causal = (row >= key)[:, :gw]
    blockdiag = bd_ref[...]
    for j in range(lb // CHUNK):
        rows = slice(j * CHUNK, (j + 1) * CHUNK)
        ce = cum_exp[rows, :]
        cum_s = jnp.sum(jnp.where(diag, ce, 0.0), axis=0, keepdims=True)
        cum_last = ce[CHUNK - 1:CHUNK, :]
        for g in range(SSM_GROUPS):
            cols = slice(g * gw, (g + 1) * gw)
            bg = x_ref[rows, b_off + g * SSM_DSTATE:b_off + (g + 1) * SSM_DSTATE]
            cg = x_ref[rows, c_off + g * SSM_DSTATE:c_off + (g + 1) * SSM_DSTATE]
            cb = _dot_nt(cg, jnp.concatenate([bg, bg], axis=0))
            cb = jnp.concatenate([cb] * (hpg // 2), axis=1)
            ceg = ce[:, cols]
            lam = jnp.exp2(jnp.where(causal, ceg - cum_s[:, cols], -jnp.inf))
            m = (cb * lam).astype(BF16)
            xg = x_ref[rows, d_inner + g * gw:d_inner + (g + 1) * gw].astype(F32)
            xdt = xg * dt_exp[rows, cols]
            xbd = jnp.concatenate([xdt.astype(BF16)] * hpg, axis=0) * blockdiag
            st = state_ref[g]
            y = _dot(m, xbd) + _dot(cg, st.astype(BF16)) * jnp.exp2(ceg) + dexp_ref[:, cols] * xg
            to_end = jnp.exp2(cum_last[:, cols] - ceg)
            state_ref[g] = jnp.exp2(cum_last[:, cols]) * st + _dot_tn(bg, (xdt * to_end).astype(BF16))
            z = x_ref[rows, cols].astype(F32)
            o_ref[rows, cols] = _rms(y * (z * jax.nn.sigmoid(z)), nw_ref[:, cols]).astype(BF16)


def _ssd(x, dt, a_log, d_exp, norm_w, d_inner):
    bsz, seq, width = x.shape
    heads = dt.shape[-1]
    assert CHUNK == SSM_HEADDIM and d_inner == heads * SSM_HEADDIM
    lb = min(SEQ_BLOCK, seq)
    gw = d_inner // SSM_GROUPS
    tblk = _chunk_tril(lb)
    expand = (jnp.arange(heads)[:, None] == jnp.arange(d_inner)[None, :] // SSM_HEADDIM).astype(BF16)
    same_head = (jnp.arange(gw)[:, None] // CHUNK == jnp.arange(gw)[None, :] // SSM_HEADDIM).astype(BF16)
    consts = [a_log, d_exp, norm_w, tblk, jnp.tile(expand, (3, 1)), jnp.tile(expand, (2, 1)), same_head]
    return pl.pallas_call(
        functools.partial(_ssd_kernel, d_inner=d_inner),
        grid=(bsz, seq // lb),
        in_specs=[pl.BlockSpec((None, lb, width), lambda b, i: (b, i, 0)),
                  pl.BlockSpec((None, lb, heads), lambda b, i: (b, i, 0))] + [_resident(c.shape) for c in consts],
        out_specs=pl.BlockSpec((None, lb, d_inner), lambda b, i: (b, i, 0)),
        out_shape=jax.ShapeDtypeStruct((bsz, seq, d_inner), BF16),
        scratch_shapes=[pltpu.VMEM((SSM_GROUPS, SSM_DSTATE, gw), F32)],
        compiler_params=_params("parallel", "arbitrary"),
        name="ssd",
    )(x, dt, *consts)


ATT_Q_CHUNKS = 4
ATT_HEAD_GROUP = 4


def _att_kernel(x_ref, bias_ref, o_ref, *, width):
    seq = x_ref.shape[0]
    pad = ATT_LEFT_CHUNKS * CHUNK
    qrows = ATT_Q_CHUNKS * CHUNK
    win = pad + qrows
    gwid = ATT_HEAD_GROUP * ATT_HD
    scale = ATT_HD ** -0.5
    head_of_lane = lax.broadcasted_iota(jnp.int32, (qrows, gwid), 1) // ATT_HD

    def block(r0, k0, nk):
        n_heads = width // ATT_HD

        def scores(head):
            hq, h = divmod(head, ATT_HEAD_GROUP)
            q = x_ref[pl.ds(r0, qrows), hq * gwid:(hq + 1) * gwid] * scale
            kb = x_ref[pl.ds(k0, nk), width + hq * gwid:width + (hq + 1) * gwid]
            return _dot_nt(jnp.where(head_of_lane == h, q, 0).astype(BF16), kb)

        nxt = scores(0)
        o = None
        for head in range(n_heads):
            hq, h = divmod(head, ATT_HEAD_GROUP)
            s = nxt
            if head + 1 < n_heads:
                nxt = scores(head + 1)
            sh = s + bias_ref[head, :, win - nk:win]
            e = jnp.exp(sh - jnp.max(sh, axis=-1, keepdims=True))
            prob = (e * (1.0 / jnp.sum(e, axis=-1, keepdims=True))).astype(BF16)
            vb = x_ref[pl.ds(k0, nk), 2 * width + hq * gwid:2 * width + (hq + 1) * gwid]
            oh = _dot(prob, vb)
            o = oh if h == 0 else jnp.where(head_of_lane == h, oh, o)
            if h == ATT_HEAD_GROUP - 1:
                o_ref[pl.ds(r0, qrows), hq * gwid:(hq + 1) * gwid] = o.astype(BF16)

    n_head = pad // qrows
    for i in range(n_head):
        block(i * qrows, 0, (i + 1) * qrows)

    def body(i, carry):
        r0 = pl.multiple_of(i * qrows, qrows)
        block(r0, pl.multiple_of(r0 - pad, qrows), win)
        return carry

    lax.fori_loop(n_head, seq // qrows, body, 0)


def _att(x, bias, width):
    bsz, seq, _ = x.shape
    assert seq % (ATT_Q_CHUNKS * CHUNK) == 0 and (ATT_LEFT_CHUNKS * CHUNK) % (ATT_Q_CHUNKS * CHUNK) == 0
    assert width % (ATT_HEAD_GROUP * ATT_HD) == 0
    return pl.pallas_call(
        functools.partial(_att_kernel, width=width),
        grid=(bsz,),
        in_specs=[pl.BlockSpec((None, seq, 3 * width), lambda b: (b, 0, 0)), _resident(bias.shape)],
        out_specs=pl.BlockSpec((None, seq, width), lambda b: (b, 0, 0)),
        out_shape=jax.ShapeDtypeStruct((bsz, seq, width), BF16),
        compiler_params=_params("parallel"),
        name="att",
    )(x, bias)


def _att_bias(rel_table):
    pad = ATT_LEFT_CHUNKS * CHUNK
    qrows = ATT_Q_CHUNKS * CHUNK
    win = pad + qrows
    n = qrows + win
    m = jnp.arange(n)
    off = jnp.where(m < win, m, m - n)
    u = rel_table[jnp.clip(pad - off, -ATT_MAX_REL, ATT_MAX_REL) + ATT_MAX_REL].astype(F32).T
    toep = jnp.tile(u, (1, qrows))[:, :qrows * (n - 1)].reshape(-1, qrows, n - 1)[:, :, :win]
    qc = jnp.arange(qrows)[:, None] // CHUNK
    kc = jnp.arange(win)[None, :] // CHUNK
    return jnp.where((kc >= qc) & (kc <= qc + ATT_LEFT_CHUNKS), toep, -jnp.inf)


def _merge_kernel(h_ref, a_ref, s_ref, c_ref, g_ref, wa_ref, ws_ref, wc_ref, wo_ref, o_ref):
    d = h_ref.shape[1]
    ya = _dot(a_ref[...], wa_ref[...])
    ys = _dot(s_ref[...], ws_ref[...])
    yc = _dot(c_ref[...], wc_ref[...])
    merged = (g_ref[:, 0:d].astype(F32) * ya + g_ref[:, d:2 * d].astype(F32) * ys
              + g_ref[:, 2 * d:3 * d].astype(F32) * yc)
    o_ref[...] = h_ref[...] + _dot(merged.astype(BF16), wo_ref[...])


def _merge(h, o_gla, o_ssm, o_att, gates, wa, ws, wc, wo):
    t, d = h.shape
    bm = min(TOKEN_BLOCK, t)

    def row(cols):
        return pl.BlockSpec((bm, cols), lambda i: (i, 0))

    return pl.pallas_call(
        _merge_kernel, grid=(t // bm,),
        in_specs=[row(d), row(o_gla.shape[1]), row(o_ssm.shape[1]), row(o_att.shape[1]), row(3 * d),
                  _resident(wa.shape), _resident(ws.shape), _resident(wc.shape), _resident(wo.shape)],
        out_specs=row(d),
        out_shape=jax.ShapeDtypeStruct((t, d), F32),
        compiler_params=_params("parallel"),
        name="merge",
    )(h, o_gla, o_ssm, o_att, gates, wa, ws, wc, wo)


def _mixer_layer(h, bsz, seq, mix_norm, w_in, gla_w_gk, gla_b_gk, gla_norm, ssm_conv_w, ssm_conv_b, ssm_dt_bias,
                 ssm_A_log, ssm_D, ssm_norm, attn_rel_bias, gate_bias, w_branch_gla, w_branch_ssm, w_branch_attn,
                 w_out):
    t, d = h.shape
    rank, gla_dk = gla_w_gk.shape
    gla_dv = w_branch_gla.shape[0]
    d_inner = w_branch_ssm.shape[0]
    conv_dim = ssm_conv_w.shape[1]
    heads = ssm_A_log.shape[0]
    att_w = w_branch_attn.shape[0]
    sizes = (gla_dk, gla_dk, gla_dv, gla_dv, rank, d_inner, conv_dim, heads, att_w, att_w, att_w, 3 * d)
    offs = [0]
    for s in sizes:
        offs.append(offs[-1] + s)
    w = w_in.astype(BF16)
    nw = mix_norm.reshape(1, d)

    qkvr, gk, qkv, gates = _proj_call(
        _proj_main_kernel, "proj_main", h,
        [nw, w[:, offs[0]:offs[4]], w[:, offs[4]:offs[5]], gla_w_gk.astype(BF16), gla_b_gk.reshape(1, gla_dk),
         w[:, offs[8]:offs[11]], w[:, offs[11]:offs[12]], gate_bias.reshape(1, 3 * d)],
        [(offs[4], BF16), (gla_dk, F32), (3 * att_w, BF16), (3 * d, BF16)])
    zx, dt = _proj_call(
        functools.partial(_proj_ssm_kernel, d_inner=d_inner, blocks_per_seq=seq // min(TOKEN_BLOCK, seq)),
        "proj_ssm", h,
        [nw, w[:, offs[5]:offs[7]], w[:, offs[7]:offs[8]], ssm_dt_bias.reshape(1, heads),
         ssm_conv_w, ssm_conv_b.reshape(1, conv_dim)],
        [(offs[7] - offs[5], BF16), (heads, F32)],
        scratch=[pltpu.VMEM((CONV_HALO, conv_dim), F32),
                 pltpu.VMEM((2, CONV_HALO + min(TOKEN_BLOCK, seq), CONV_COL_TILE), F32),
                 pltpu.VMEM((min(TOKEN_BLOCK, seq), d), BF16)], carries=True)
    o_gla = _gla(qkvr.reshape(bsz, seq, -1), gk.reshape(bsz, seq, gla_dk),
                 gla_norm.reshape(1, -1), gla_dk, gla_dv)
    o_ssm = _ssd(zx.reshape(bsz, seq, -1), dt.reshape(bsz, seq, heads), ssm_A_log.reshape(1, heads),
                 jnp.repeat(ssm_D, SSM_HEADDIM).reshape(1, d_inner), ssm_norm.reshape(1, d_inner), d_inner)
    o_att = _att(qkv.reshape(bsz, seq, -1), _att_bias(attn_rel_bias), att_w)
    return _merge(h, o_gla.reshape(t, -1), o_ssm.reshape(t, -1), o_att.reshape(t, -1), gates,
                  w_branch_gla.astype(BF16), w_branch_ssm.astype(BF16), w_branch_attn.astype(BF16),
                  w_out.astype(BF16))


def kernel(x, ffn1_norm, ffn1_w_gate, ffn1_w_up, ffn1_w_down, mix_norm, w_in, gla_w_gk, gla_b_gk, gla_norm, ssm_conv_w, ssm_conv_b, ssm_dt_bias, ssm_A_log, ssm_D, ssm_norm, attn_rel_bias, gate_bias, w_branch_gla, w_branch_ssm, w_branch_attn, w_out, ffn2_norm, ffn2_w_gate, ffn2_w_up, ffn2_w_down, final_norm):
    bsz, seq, d = x.shape
    depth = w_in.shape[0]
    h = x.reshape(bsz * seq, d)
    fw = final_norm.reshape(1, d)
    for l in range(depth):
        h = _ffn(h, ffn1_norm[l].reshape(1, d), ffn1_w_gate[l].astype(BF16), ffn1_w_up[l].astype(BF16),
                 ffn1_w_down[l].astype(BF16), fw, False)
        h = _mixer_layer(h, bsz, seq, mix_norm[l], w_in[l], gla_w_gk[l], gla_b_gk[l], gla_norm[l], ssm_conv_w[l],
                         ssm_conv_b[l], ssm_dt_bias[l], ssm_A_log[l], ssm_D[l], ssm_norm[l], attn_rel_bias[l],
                         gate_bias[l], w_branch_gla[l], w_branch_ssm[l], w_branch_attn[l], w_out[l])
        h = _ffn(h, ffn2_norm[l].reshape(1, d), ffn2_w_gate[l].astype(BF16), ffn2_w_up[l].astype(BF16),
                 ffn2_w_down[l].astype(BF16), fw, l == depth - 1)
    return h.reshape(bsz, seq, d)
```

```python
import functools

import jax
import jax.numpy as jnp
from jax import lax
from jax.experimental import pallas as pl
from jax.experimental.pallas import tpu as pltpu

F32 = jnp.float32
BF16 = jnp.bfloat16

EPS = 1e-6
LOG2_E = 1.4426950408889634
CHUNK = 64
GLA_HEADS = 4
GLA_GATE_NORM = 16.0
SSM_HEADDIM = 64
SSM_GROUPS = 8
SSM_DSTATE = 128
SSM_CONV = 4
ATT_HD = 64
ATT_LEFT_CHUNKS = 8
ATT_MAX_REL = 256

V7X_VMEM_BYTES = 64 * 1024 * 1024
VMEM_LIMIT_BYTES = V7X_VMEM_BYTES - 8 * 1024 * 1024
LANES = 128
MXU_TILE = 256

TOKEN_BLOCK = 512
FFN_TOKEN_BLOCK = 1024
SSM_TOKEN_BLOCK = 1024
SEQ_BLOCK = 512
GLA_SEQ_BLOCK = 512


def _params(*semantics):
    return pltpu.CompilerParams(dimension_semantics=semantics, vmem_limit_bytes=VMEM_LIMIT_BYTES)


def _resident(shape):
    zeros = (0,) * len(shape)
    return pl.BlockSpec(shape, lambda *_: zeros, pipeline_mode=pl.Buffered(1))


def _dot(a, b):
    return jnp.dot(a, b, preferred_element_type=F32)


def _dot_nt(a, b):
    return lax.dot_general(a, b, (((1,), (1,)), ((), ())), preferred_element_type=F32)


def _dot_tn(a, b):
    return lax.dot_general(a, b, (((0,), (0,)), ((), ())), preferred_element_type=F32)


def _rms(x, w):
    return x * lax.rsqrt(jnp.mean(x * x, axis=-1, keepdims=True) + EPS) * w


def _split3(x):
    hi = x.astype(BF16)
    r = x - hi.astype(F32)
    mid = r.astype(BF16)
    lo = (r - mid.astype(F32)).astype(BF16)
    return hi, mid, lo


def _sel_left(sel, x):
    hi, mid, lo = _split3(x)
    return _dot(sel, hi) + _dot(sel, mid) + _dot(sel, lo)


def _tril(n):
    return lax.broadcasted_iota(jnp.int32, (n, n), 0) >= lax.broadcasted_iota(jnp.int32, (n, n), 1)


def _ffn_kernel(h_ref, nw_ref, wg_ref, wu_ref, wd_ref, fw_ref, o_ref, *, final_norm):
    x = h_ref[...]
    xn = _rms(x, nw_ref[...]).astype(BF16)
    fc = MXU_TILE
    f_chunks = wg_ref.shape[1] // fc

    def gate_up(i):
        return _dot(xn, wg_ref[:, i * fc:(i + 1) * fc]), _dot(xn, wu_ref[:, i * fc:(i + 1) * fc])

    acc = None
    nxt = gate_up(0)
    for i in range(f_chunks):
        g, u = nxt
        if i + 1 < f_chunks:
            nxt = gate_up(i + 1)
        a = (g * jax.nn.sigmoid(g) * u).astype(BF16)
        y = _dot(a, wd_ref[i * fc:(i + 1) * fc, :])
        acc = y if acc is None else acc + y
    out = x + 0.5 * acc
    if final_norm:
        out = _rms(out, fw_ref[...])
    o_ref[...] = out


def _ffn(h, norm_w, wg, wu, wd, final_w, final_norm):
    t, d = h.shape
    f_dim = wg.shape[1]
    assert f_dim % MXU_TILE == 0
    bm = min(FFN_TOKEN_BLOCK, t)
    row = pl.BlockSpec((bm, d), lambda i: (i, 0))
    return pl.pallas_call(
        functools.partial(_ffn_kernel, final_norm=final_norm),
        grid=(t // bm,),
        in_specs=[row, _resident((1, d)), _resident((d, f_dim)), _resident((d, f_dim)),
                  _resident((f_dim, d)), _resident((1, d))],
        out_specs=row,
        out_shape=jax.ShapeDtypeStruct((t, d), F32),
        compiler_params=_params("parallel"),
        name="ffn",
    )(h, norm_w, wg, wu, wd, final_w)


def _proj_main_kernel(h_ref, nw_ref, wgla_ref, wc_ref, wgk_ref, bgk_ref, watt_ref, wgate_ref, bgate_ref,
                      gla_ref, gk_ref, att_ref, gate_ref):
    xn = _rms(h_ref[...], nw_ref[...]).astype(BF16)
    gla_ref[...] = _dot(xn, wgla_ref[...]).astype(BF16)
    code = _dot(xn, wc_ref[...]).astype(BF16)
    pre = _dot(code, wgk_ref[...]) + bgk_ref[...]
    gk_ref[...] = jax.nn.log_sigmoid(pre) * (LOG2_E / GLA_GATE_NORM)
    att_ref[...] = _dot(xn, watt_ref[...]).astype(BF16)
    gate_ref[...] = jax.nn.sigmoid(_dot(xn, wgate_ref[...]) + bgate_ref[...]).astype(BF16)


CONV_HALO = 8
CONV_COL_TILE = 512
CONV_ROW_TILE = 64


def _proj_ssm_kernel(h_ref, nw_ref, w_ref, wdt_ref, dtb_ref, cw_ref, cb_ref, o_ref, dt_ref, carry_ref, acc_ref,
                     xn_ref, *, d_inner, blocks_per_seq):
    assert SSM_CONV == 4
    bm = h_ref.shape[0]

    @pl.when(pl.program_id(0) % blocks_per_seq == 0)
    def _():
        carry_ref[...] = jnp.zeros_like(carry_ref)

    xn_ref[...] = _rms(h_ref[...], nw_ref[...]).astype(BF16)
    dt_ref[...] = jax.nn.softplus(_dot(xn_ref[...], wdt_ref[...]) + dtb_ref[...])
    n_conv = (w_ref.shape[1] - d_inner) // CONV_COL_TILE
    n_tiles = n_conv + d_inner // CONV_COL_TILE

    def matmul_tile(k):
        if k < n_conv:
            c0 = k * CONV_COL_TILE
            acc_ref[k % 2, 0:CONV_HALO, :] = carry_ref[:, c0:c0 + CONV_COL_TILE]
            acc_ref[k % 2, CONV_HALO:, :] = _dot(xn_ref[...], w_ref[:, d_inner + c0:d_inner + c0 + CONV_COL_TILE])
        else:
            c0 = (k - n_conv) * CONV_COL_TILE
            o_ref[:, c0:c0 + CONV_COL_TILE] = _dot(xn_ref[...], w_ref[:, c0:c0 + CONV_COL_TILE]).astype(BF16)

    def conv_tile(k):
        c0 = k * CONV_COL_TILE
        for r0 in range(0, bm, CONV_ROW_TILE):
            for l0 in range(0, CONV_COL_TILE, LANES):
                cols = slice(c0 + l0, c0 + l0 + LANES)
                x0 = acc_ref[k % 2, r0:r0 + CONV_HALO + CONV_ROW_TILE, l0:l0 + LANES]
                x1 = pltpu.roll(x0, 1, axis=0)
                near = cw_ref[3:4, cols] * x0 + cw_ref[2:3, cols] * x1
                far = pltpu.roll(cw_ref[1:2, cols] * x0 + cw_ref[0:1, cols] * x1, 2, axis=0)
                conv = (cb_ref[:, cols] + near + far)[CONV_HALO:, :]
                o_ref[r0:r0 + CONV_ROW_TILE, d_inner + c0 + l0:d_inner + c0 + l0 + LANES] = (
                    conv * jax.nn.sigmoid(conv)).astype(BF16)
        carry_ref[:, c0:c0 + CONV_COL_TILE] = acc_ref[k % 2, bm:bm + CONV_HALO, :]

    matmul_tile(0)
    for k in range(n_tiles):
        if k + 1 < n_tiles:
            matmul_tile(k + 1)
        if k < n_conv:
            conv_tile(k)


def _proj_call(kernel, name, h, consts, outs, scratch=(), carries=False, block=TOKEN_BLOCK):
    t, d = h.shape
    bm = min(block, t)
    in_specs = [pl.BlockSpec((bm, d), lambda i: (i, 0))] + [_resident(c.shape) for c in consts]
    out_specs = [pl.BlockSpec((bm, cols), lambda i: (i, 0)) for cols, _ in outs]
    out_shape = [jax.ShapeDtypeStruct((t, cols), dtype) for cols, dtype in outs]
    return pl.pallas_call(
        kernel, grid=(t // bm,), in_specs=in_specs, out_specs=out_specs, out_shape=out_shape,
        scratch_shapes=list(scratch),
        compiler_params=_params("arbitrary" if carries else "parallel"), name=name,
    )(h, *consts)


def _gla_kernel(x_ref, gk_ref, nw_ref, o_ref, state_ref, *, dk, dv):
    hk, hv = dk // GLA_HEADS, dv // GLA_HEADS
    scale = hk ** -0.5

    @pl.when(pl.program_id(1) == 0)
    def _():
        state_ref[...] = jnp.zeros_like(state_ref)

    tril = _tril(CHUNK)
    tril_b = tril.astype(BF16)
    nw = nw_ref[...]
    for j in range(x_ref.shape[0] // CHUNK):
        rows = slice(j * CHUNK, (j + 1) * CHUNK)
        b_all = _sel_left(tril_b, gk_ref[rows, :])
        for h in range(GLA_HEADS):
            q = x_ref[rows, h * hk:(h + 1) * hk].astype(F32) * scale
            k = x_ref[rows, dk + h * hk:dk + (h + 1) * hk].astype(F32)
            v = x_ref[rows, 2 * dk + h * hv:2 * dk + (h + 1) * hv]
            r = x_ref[rows, 2 * dk + dv + h * hv:2 * dk + dv + (h + 1) * hv].astype(F32)
            b = b_all[:, h * hk:(h + 1) * hk]
            b_last = b[CHUNK - 1:CHUNK, :]
            b_mid = b[CHUNK // 2:CHUNK // 2 + 1, :]
            a = _dot_nt((q * jnp.exp2(b - b_mid)).astype(BF16), (k * jnp.exp2(b_mid - b)).astype(BF16))
            a = jnp.where(tril, a, 0.0)
            st = state_ref[h]
            o = _dot(a.astype(BF16), v) + _dot_nt((q * jnp.exp2(b)).astype(BF16), st.astype(BF16))
            state_ref[h] = jnp.exp2(b_last) * st + _dot_tn(v, (k * jnp.exp2(b_last - b)).astype(BF16))
            o = _rms(o, nw) * (r * jax.nn.sigmoid(r))
            o_ref[rows, h * hv:(h + 1) * hv] = o.astype(BF16)


def _chunk_tril(n):
    pos = jnp.arange(n)
    return ((pos[:, None] >= pos[None, :]) & (pos[:, None] // CHUNK == pos[None, :] // CHUNK)).astype(BF16)


def _gla(x, gk, norm_w, dk, dv):
    bsz, seq, width = x.shape
    lb = min(GLA_SEQ_BLOCK, seq)
    hk, hv = dk // GLA_HEADS, dv // GLA_HEADS
    return pl.pallas_call(
        functools.partial(_gla_kernel, dk=dk, dv=dv),
        grid=(bsz, seq // lb),
        in_specs=[pl.BlockSpec((None, lb, width), lambda b, i: (b, i, 0)),
                  pl.BlockSpec((None, lb, dk), lambda b, i: (b, i, 0)),
                  _resident((1, hv))],
        out_specs=pl.BlockSpec((None, lb, dv), lambda b, i: (b, i, 0)),
        out_shape=jax.ShapeDtypeStruct((bsz, seq, dv), BF16),
        scratch_shapes=[pltpu.VMEM((GLA_HEADS, hv, hk), F32)],
        compiler_params=_params("parallel", "arbitrary"),
        name="gla",
    )(x, gk, norm_w)


def _ssd_kernel(x_ref, dt_ref, alog_ref, dexp_ref, nw_ref, tblk_ref, e3_ref, e2_ref, bd_ref,
                o_ref, state_ref, *, d_inner):
    lb = x_ref.shape[0]
    gn = SSM_GROUPS * SSM_DSTATE
    gw = d_inner // SSM_GROUPS
    hpg = gw // SSM_HEADDIM
    b_off, c_off = 2 * d_inner, 2 * d_inner + gn

    @pl.when(pl.program_id(1) == 0)
    def _():
        state_ref[...] = jnp.zeros_like(state_ref)

    dt = dt_ref[...]
    cum = _sel_left(tblk_ref[...], dt * (-LOG2_E * jnp.exp(alog_ref[...])))
    cum_exp = _dot(jnp.concatenate(_split3(cum), axis=1), e3_ref[...])
    dt_hi, dt_mid, _ = _split3(dt)
    dt_exp = _dot(jnp.concatenate([dt_hi, dt_mid], axis=1), e2_ref[...])

    row = lax.broadcasted_iota(jnp.int32, (CHUNK, d_inner), 0)
    key = lax.rem(lax.broadcasted_iota(jnp.int32, (CHUNK, d_inner), 1), SSM_HEADDIM)
    diag = row == key
    causal = (row >= key)[:, :gw]
    blockdiag = bd_ref[...]
    for j in range(lb // CHUNK):
        rows = slice(j * CHUNK, (j + 1) * CHUNK)
        ce = cum_exp[rows, :]
        cum_s = jnp.sum(jnp.where(diag, ce, 0.0), axis=0, keepdims=True)
        cum_last = ce[CHUNK - 1:CHUNK, :]
        for g in range(SSM_GROUPS):
            cols = slice(g * gw, (g + 1) * gw)
            bg = x_ref[rows, b_off + g * SSM_DSTATE:b_off + (g + 1) * SSM_DSTATE]
            cg = x_ref[rows, c_off + g * SSM_DSTATE:c_off + (g + 1) * SSM_DSTATE]
            cb = _dot_nt(cg, jnp.concatenate([bg, bg], axis=0))
            cb = jnp.concatenate([cb] * (hpg // 2), axis=1)
            ceg = ce[:, cols]
            lam = jnp.exp2(jnp.where(causal, ceg - cum_s[:, cols], -jnp.inf))
            m = (cb * lam).astype(BF16)
            xg = x_ref[rows, d_inner + g * gw:d_inner + (g + 1) * gw].astype(F32)
            xdt = xg * dt_exp[rows, cols]
            xbd = jnp.concatenate([xdt.astype(BF16)] * hpg, axis=0) * blockdiag
            st = state_ref[g]
            y = _dot(m, xbd) + _dot(cg, st.astype(BF16)) * jnp.exp2(ceg) + dexp_ref[:, cols] * xg
            to_end = jnp.exp2(cum_last[:, cols] - ceg)
            state_ref[g] = jnp.exp2(cum_last[:, cols]) * st + _dot_tn(bg, (xdt * to_end).astype(BF16))
            z = x_ref[rows, cols].astype(F32)
            o_ref[rows, cols] = _rms(y * (z * jax.nn.sigmoid(z)), nw_ref[:, cols]).astype(BF16)


def _ssd(x, dt, a_log, d_exp, norm_w, d_inner):
    bsz, seq, width = x.shape
    heads = dt.shape[-1]
    assert CHUNK == SSM_HEADDIM and d_inner == heads * SSM_HEADDIM
    lb = min(SEQ_BLOCK, seq)
    gw = d_inner // SSM_GROUPS
    tblk = _chunk_tril(lb)
    expand = (jnp.arange(heads)[:, None] == jnp.arange(d_inner)[None, :] // SSM_HEADDIM).astype(BF16)
    same_head = (jnp.arange(gw)[:, None] // CHUNK == jnp.arange(gw)[None, :] // SSM_HEADDIM).astype(BF16)
    consts = [a_log, d_exp, norm_w, tblk, jnp.tile(expand, (3, 1)), jnp.tile(expand, (2, 1)), same_head]
    return pl.pallas_call(
        functools.partial(_ssd_kernel, d_inner=d_inner),
        grid=(bsz, seq // lb),
        in_specs=[pl.BlockSpec((None, lb, width), lambda b, i: (b, i, 0)),
                  pl.BlockSpec((None, lb, heads), lambda b, i: (b, i, 0))] + [_resident(c.shape) for c in consts],
        out_specs=pl.BlockSpec((None, lb, d_inner), lambda b, i: (b, i, 0)),
        out_shape=jax.ShapeDtypeStruct((bsz, seq, d_inner), BF16),
        scratch_shapes=[pltpu.VMEM((SSM_GROUPS, SSM_DSTATE, gw), F32)],
        compiler_params=_params("parallel", "arbitrary"),
        name="ssd",
    )(x, dt, *consts)


ATT_Q_CHUNKS = 4
ATT_HEAD_GROUP = 4


def _att_kernel(x_ref, bias_ref, o_ref, *, width):
    seq = x_ref.shape[0]
    pad = ATT_LEFT_CHUNKS * CHUNK
    qrows = ATT_Q_CHUNKS * CHUNK
    win = pad + qrows
    gwid = ATT_HEAD_GROUP * ATT_HD
    scale = ATT_HD ** -0.5
    head_of_lane = lax.broadcasted_iota(jnp.int32, (qrows, gwid), 1) // ATT_HD

    def block(r0, k0, nk):
        n_heads = width // ATT_HD

        def scores(head):
            hq, h = divmod(head, ATT_HEAD_GROUP)
            q = x_ref[pl.ds(r0, qrows), hq * gwid:(hq + 1) * gwid] * scale
            kb = x_ref[pl.ds(k0, nk), width + hq * gwid:width + (hq + 1) * gwid]
            return _dot_nt(jnp.where(head_of_lane == h, q, 0).astype(BF16), kb)

        nxt = scores(0)
        o = None
        for head in range(n_heads):
            hq, h = divmod(head, ATT_HEAD_GROUP)
            s = nxt
            if head + 1 < n_heads:
                nxt = scores(head + 1)
            sh = s + bias_ref[head, :, win - nk:win]
            e = jnp.exp(sh - jnp.max(sh, axis=-1, keepdims=True))
            prob = (e * (1.0 / jnp.sum(e, axis=-1, keepdims=True))).astype(BF16)
            vb = x_ref[pl.ds(k0, nk), 2 * width + hq * gwid:2 * width + (hq + 1) * gwid]
            oh = _dot(prob, vb)
            o = oh if h == 0 else jnp.where(head_of_lane == h, oh, o)
            if h == ATT_HEAD_GROUP - 1:
                o_ref[pl.ds(r0, qrows), hq * gwid:(hq + 1) * gwid] = o.astype(BF16)

    n_head = pad // qrows
    for i in range(n_head):
        block(i * qrows, 0, (i + 1) * qrows)

    def body(i, carry):
        r0 = pl.multiple_of(i * qrows, qrows)
        block(r0, pl.multiple_of(r0 - pad, qrows), win)
        return carry

    lax.fori_loop(n_head, seq // qrows, body, 0)


def _att(x, bias, width):
    bsz, seq, _ = x.shape
    assert seq % (ATT_Q_CHUNKS * CHUNK) == 0 and (ATT_LEFT_CHUNKS * CHUNK) % (ATT_Q_CHUNKS * CHUNK) == 0
    assert width % (ATT_HEAD_GROUP * ATT_HD) == 0
    return pl.pallas_call(
        functools.partial(_att_kernel, width=width),
        grid=(bsz,),
        in_specs=[pl.BlockSpec((None, seq, 3 * width), lambda b: (b, 0, 0)), _resident(bias.shape)],
        out_specs=pl.BlockSpec((None, seq, width), lambda b: (b, 0, 0)),
        out_shape=jax.ShapeDtypeStruct((bsz, seq, width), BF16),
        compiler_params=_params("parallel"),
        name="att",
    )(x, bias)


def _att_bias(rel_table):
    pad = ATT_LEFT_CHUNKS * CHUNK
    qrows = ATT_Q_CHUNKS * CHUNK
    win = pad + qrows
    n = qrows + win
    m = jnp.arange(n)
    off = jnp.where(m < win, m, m - n)
    u = rel_table[jnp.clip(pad - off, -ATT_MAX_REL, ATT_MAX_REL) + ATT_MAX_REL].astype(F32).T
    toep = jnp.tile(u, (1, qrows))[:, :qrows * (n - 1)].reshape(-1, qrows, n - 1)[:, :, :win]
    qc = jnp.arange(qrows)[:, None] // CHUNK
    kc = jnp.arange(win)[None, :] // CHUNK
    return jnp.where((kc >= qc) & (kc <= qc + ATT_LEFT_CHUNKS), toep, -jnp.inf)


def _merge_kernel(h_ref, a_ref, s_ref, c_ref, g_ref, wa_ref, ws_ref, wc_ref, wo_ref, o_ref):
    d = h_ref.shape[1]
    ya = _dot(a_ref[...], wa_ref[...])
    ys = _dot(s_ref[...], ws_ref[...])
    yc = _dot(c_ref[...], wc_ref[...])
    merged = (g_ref[:, 0:d].astype(F32) * ya + g_ref[:, d:2 * d].astype(F32) * ys
              + g_ref[:, 2 * d:3 * d].astype(F32) * yc)
    o_ref[...] = h_ref[...] + _dot(merged.astype(BF16), wo_ref[...])


def _merge(h, o_gla, o_ssm, o_att, gates, wa, ws, wc, wo):
    t, d = h.shape
    bm = min(TOKEN_BLOCK, t)

    def row(cols):
        return pl.BlockSpec((bm, cols), lambda i: (i, 0))

    return pl.pallas_call(
        _merge_kernel, grid=(t // bm,),
        in_specs=[row(d), row(o_gla.shape[1]), row(o_ssm.shape[1]), row(o_att.shape[1]), row(3 * d),
                  _resident(wa.shape), _resident(ws.shape), _resident(wc.shape), _resident(wo.shape)],
        out_specs=row(d),
        out_shape=jax.ShapeDtypeStruct((t, d), F32),
        compiler_params=_params("parallel"),
        name="merge",
    )(h, o_gla, o_ssm, o_att, gates, wa, ws, wc, wo)


def _mixer_layer(h, bsz, seq, mix_norm, w_in, gla_w_gk, gla_b_gk, gla_norm, ssm_conv_w, ssm_conv_b, ssm_dt_bias,
                 ssm_A_log, ssm_D, ssm_norm, attn_rel_bias, gate_bias, w_branch_gla, w_branch_ssm, w_branch_attn,
                 w_out):
    t, d = h.shape
    rank, gla_dk = gla_w_gk.shape
    gla_dv = w_branch_gla.shape[0]
    d_inner = w_branch_ssm.shape[0]
    conv_dim = ssm_conv_w.shape[1]
    heads = ssm_A_log.shape[0]
    att_w = w_branch_attn.shape[0]
    sizes = (gla_dk, gla_dk, gla_dv, gla_dv, rank, d_inner, conv_dim, heads, att_w, att_w, att_w, 3 * d)
    offs = [0]
    for s in sizes:
        offs.append(offs[-1] + s)
    w = w_in.astype(BF16)
    nw = mix_norm.reshape(1, d)

    qkvr, gk, qkv, gates = _proj_call(
        _proj_main_kernel, "proj_main", h,
        [nw, w[:, offs[0]:offs[4]], w[:, offs[4]:offs[5]], gla_w_gk.astype(BF16), gla_b_gk.reshape(1, gla_dk),
         w[:, offs[8]:offs[11]], w[:, offs[11]:offs[12]], gate_bias.reshape(1, 3 * d)],
        [(offs[4], BF16), (gla_dk, F32), (3 * att_w, BF16), (3 * d, BF16)])
    zx, dt = _proj_call(
        functools.partial(_proj_ssm_kernel, d_inner=d_inner, blocks_per_seq=seq // min(SSM_TOKEN_BLOCK, seq)),
        "proj_ssm", h,
        [nw, w[:, offs[5]:offs[7]], w[:, offs[7]:offs[8]], ssm_dt_bias.reshape(1, heads),
         ssm_conv_w, ssm_conv_b.reshape(1, conv_dim)],
        [(offs[7] - offs[5], BF16), (heads, F32)],
        scratch=[pltpu.VMEM((CONV_HALO, conv_dim), F32),
                 pltpu.VMEM((2, CONV_HALO + min(SSM_TOKEN_BLOCK, seq), CONV_COL_TILE), F32),
                 pltpu.VMEM((min(SSM_TOKEN_BLOCK, seq), d), BF16)], carries=True, block=SSM_TOKEN_BLOCK)
    o_gla = _gla(qkvr.reshape(bsz, seq, -1), gk.reshape(bsz, seq, gla_dk),
                 gla_norm.reshape(1, -1), gla_dk, gla_dv)
    o_ssm = _ssd(zx.reshape(bsz, seq, -1), dt.reshape(bsz, seq, heads), ssm_A_log.reshape(1, heads),
                 jnp.repeat(ssm_D, SSM_HEADDIM).reshape(1, d_inner), ssm_norm.reshape(1, d_inner), d_inner)
    o_att = _att(qkv.reshape(bsz, seq, -1), _att_bias(attn_rel_bias), att_w)
    return _merge(h, o_gla.reshape(t, -1), o_ssm.reshape(t, -1), o_att.reshape(t, -1), gates,
                  w_branch_gla.astype(BF16), w_branch_ssm.astype(BF16), w_branch_attn.astype(BF16),
                  w_out.astype(BF16))


def kernel(x, ffn1_norm, ffn1_w_gate, ffn1_w_up, ffn1_w_down, mix_norm, w_in, gla_w_gk, gla_b_gk, gla_norm, ssm_conv_w, ssm_conv_b, ssm_dt_bias, ssm_A_log, ssm_D, ssm_norm, attn_rel_bias, gate_bias, w_branch_gla, w_branch_ssm, w_branch_attn, w_out, ffn2_norm, ffn2_w_gate, ffn2_w_up, ffn2_w_down, final_norm):
    bsz, seq, d = x.shape
    depth = w_in.shape[0]
    h = x.reshape(bsz * seq, d)
    fw = final_norm.reshape(1, d)
    for l in range(depth):
        h = _ffn(h, ffn1_norm[l].reshape(1, d), ffn1_w_gate[l].astype(BF16), ffn1_w_up[l].astype(BF16),
                 ffn1_w_down[l].astype(BF16), fw, False)
        h = _mixer_layer(h, bsz, seq, mix_norm[l], w_in[l], gla_w_gk[l], gla_b_gk[l], gla_norm[l], ssm_conv_w[l],
                         ssm_conv_b[l], ssm_dt_bias[l], ssm_A_log[l], ssm_D[l], ssm_norm[l], attn_rel_bias[l],
                         gate_bias[l], w_branch_gla[l], w_branch_ssm[l], w_branch_attn[l], w_out[l])
        h = _ffn(h, ffn2_norm[l].reshape(1, d), ffn2_w_gate[l].astype(BF16), ffn2_w_up[l].astype(BF16),
                 ffn2_w_down[l].astype(BF16), fw, l == depth - 1)
    return h.reshape(bsz, seq, d)
```

```python
import functools

import jax
import jax.numpy as jnp
from jax import lax
from jax.experimental import pallas as pl
from jax.experimental.pallas import tpu as pltpu

F32 = jnp.float32
BF16 = jnp.bfloat16

EPS = 1e-6
LOG2_E = 1.4426950408889634
CHUNK = 64
GLA_HEADS = 4
GLA_GATE_NORM = 16.0
SSM_HEADDIM = 64
SSM_GROUPS = 8
SSM_DSTATE = 128
SSM_CONV = 4
ATT_HD = 64
ATT_LEFT_CHUNKS = 8
ATT_MAX_REL = 256

V7X_VMEM_BYTES = 64 * 1024 * 1024
VMEM_LIMIT_BYTES = V7X_VMEM_BYTES - 8 * 1024 * 1024
LANES = 128
MXU_TILE = 256

TOKEN_BLOCK = 512
FFN_TOKEN_BLOCK = 1024
SSM_TOKEN_BLOCK = 1024
SEQ_BLOCK = 512
GLA_SEQ_BLOCK = 512


def _params(*semantics):
    return pltpu.CompilerParams(dimension_semantics=semantics, vmem_limit_bytes=VMEM_LIMIT_BYTES)


def _resident(shape):
    zeros = (0,) * len(shape)
    return pl.BlockSpec(shape, lambda *_: zeros, pipeline_mode=pl.Buffered(1))


def _dot(a, b):
    return jnp.dot(a, b, preferred_element_type=F32)


def _dot_nt(a, b):
    return lax.dot_general(a, b, (((1,), (1,)), ((), ())), preferred_element_type=F32)


def _dot_tn(a, b):
    return lax.dot_general(a, b, (((0,), (0,)), ((), ())), preferred_element_type=F32)


def _rms(x, w):
    return x * lax.rsqrt(jnp.mean(x * x, axis=-1, keepdims=True) + EPS) * w


def _split3(x):
    hi = x.astype(BF16)
    r = x - hi.astype(F32)
    mid = r.astype(BF16)
    lo = (r - mid.astype(F32)).astype(BF16)
    return hi, mid, lo


def _sel_left(sel, x):
    hi, mid, lo = _split3(x)
    return _dot(sel, hi) + _dot(sel, mid) + _dot(sel, lo)


def _tril(n):
    return lax.broadcasted_iota(jnp.int32, (n, n), 0) >= lax.broadcasted_iota(jnp.int32, (n, n), 1)


def _ffn_kernel(h_ref, nw_ref, wg_ref, wu_ref, wd_ref, fw_ref, o_ref, *, final_norm):
    x = h_ref[...]
    xn = _rms(x, nw_ref[...]).astype(BF16)
    fc = MXU_TILE
    f_chunks = wg_ref.shape[1] // fc

    def gate_up(i):
        return _dot(xn, wg_ref[:, i * fc:(i + 1) * fc]), _dot(xn, wu_ref[:, i * fc:(i + 1) * fc])

    acc = None
    nxt = gate_up(0)
    for i in range(f_chunks):
        g, u = nxt
        if i + 1 < f_chunks:
            nxt = gate_up(i + 1)
        a = (g * jax.nn.sigmoid(g) * u).astype(BF16)
        y = _dot(a, wd_ref[i * fc:(i + 1) * fc, :])
        acc = y if acc is None else acc + y
    out = x + 0.5 * acc
    if final_norm:
        out = _rms(out, fw_ref[...])
    o_ref[...] = out


def _ffn(h, norm_w, wg, wu, wd, final_w, final_norm):
    t, d = h.shape
    f_dim = wg.shape[1]
    assert f_dim % MXU_TILE == 0
    bm = min(FFN_TOKEN_BLOCK, t)
    row = pl.BlockSpec((bm, d), lambda i: (i, 0))
    return pl.pallas_call(
        functools.partial(_ffn_kernel, final_norm=final_norm),
        grid=(t // bm,),
        in_specs=[row, _resident((1, d)), _resident((d, f_dim)), _resident((d, f_dim)),
                  _resident((f_dim, d)), _resident((1, d))],
        out_specs=row,
        out_shape=jax.ShapeDtypeStruct((t, d), F32),
        compiler_params=_params("parallel"),
        name="ffn",
    )(h, norm_w, wg, wu, wd, final_w)


def _proj_main_kernel(h_ref, nw_ref, wgla_ref, wc_ref, wgk_ref, bgk_ref, watt_ref, wgate_ref, bgate_ref,
                      gla_ref, gk_ref, att_ref, gate_ref):
    xn = _rms(h_ref[...], nw_ref[...]).astype(BF16)
    gla_ref[...] = _dot(xn, wgla_ref[...]).astype(BF16)
    code = _dot(xn, wc_ref[...]).astype(BF16)
    pre = _dot(code, wgk_ref[...]) + bgk_ref[...]
    log_sig = jnp.minimum(pre, 0.0) - jnp.log(1.0 + jnp.exp(-jnp.abs(pre)))
    gk_ref[...] = log_sig * (LOG2_E / GLA_GATE_NORM)
    att_ref[...] = _dot(xn, watt_ref[...]).astype(BF16)
    gate_ref[...] = jax.nn.sigmoid(_dot(xn, wgate_ref[...]) + bgate_ref[...]).astype(BF16)


CONV_HALO = 8
CONV_COL_TILE = 512
CONV_ROW_TILE = 64


def _proj_ssm_kernel(h_ref, nw_ref, w_ref, wdt_ref, dtb_ref, cw_ref, cb_ref, o_ref, dt_ref, carry_ref, acc_ref,
                     xn_ref, *, d_inner, blocks_per_seq):
    assert SSM_CONV == 4
    bm = h_ref.shape[0]

    @pl.when(pl.program_id(0) % blocks_per_seq == 0)
    def _():
        carry_ref[...] = jnp.zeros_like(carry_ref)

    xn_ref[...] = _rms(h_ref[...], nw_ref[...]).astype(BF16)
    dt_ref[...] = jax.nn.softplus(_dot(xn_ref[...], wdt_ref[...]) + dtb_ref[...])
    n_conv = (w_ref.shape[1] - d_inner) // CONV_COL_TILE
    n_tiles = n_conv + d_inner // CONV_COL_TILE

    def matmul_tile(k):
        if k < n_conv:
            c0 = k * CONV_COL_TILE
            acc_ref[k % 2, 0:CONV_HALO, :] = carry_ref[:, c0:c0 + CONV_COL_TILE]
            acc_ref[k % 2, CONV_HALO:, :] = _dot(xn_ref[...], w_ref[:, d_inner + c0:d_inner + c0 + CONV_COL_TILE])
        else:
            c0 = (k - n_conv) * CONV_COL_TILE
            o_ref[:, c0:c0 + CONV_COL_TILE] = _dot(xn_ref[...], w_ref[:, c0:c0 + CONV_COL_TILE]).astype(BF16)

    def conv_tile(k):
        c0 = k * CONV_COL_TILE
        for r0 in range(0, bm, CONV_ROW_TILE):
            for l0 in range(0, CONV_COL_TILE, LANES):
                cols = slice(c0 + l0, c0 + l0 + LANES)
                x0 = acc_ref[k % 2, r0:r0 + CONV_HALO + CONV_ROW_TILE, l0:l0 + LANES]
                x1 = pltpu.roll(x0, 1, axis=0)
                near = cw_ref[3:4, cols] * x0 + cw_ref[2:3, cols] * x1
                far = pltpu.roll(cw_ref[1:2, cols] * x0 + cw_ref[0:1, cols] * x1, 2, axis=0)
                conv = (cb_ref[:, cols] + near + far)[CONV_HALO:, :]
                o_ref[r0:r0 + CONV_ROW_TILE, d_inner + c0 + l0:d_inner + c0 + l0 + LANES] = (
                    conv * jax.nn.sigmoid(conv)).astype(BF16)
        carry_ref[:, c0:c0 + CONV_COL_TILE] = acc_ref[k % 2, bm:bm + CONV_HALO, :]

    matmul_tile(0)
    for k in range(n_tiles):
        if k + 1 < n_tiles:
            matmul_tile(k + 1)
        if k < n_conv:
            conv_tile(k)


def _proj_call(kernel, name, h, consts, outs, scratch=(), carries=False, block=TOKEN_BLOCK):
    t, d = h.shape
    bm = min(block, t)
    in_specs = [pl.BlockSpec((bm, d), lambda i: (i, 0))] + [_resident(c.shape) for c in consts]
    out_specs = [pl.BlockSpec((bm, cols), lambda i: (i, 0)) for cols, _ in outs]
    out_shape = [jax.ShapeDtypeStruct((t, cols), dtype) for cols, dtype in outs]
    return pl.pallas_call(
        kernel, grid=(t // bm,), in_specs=in_specs, out_specs=out_specs, out_shape=out_shape,
        scratch_shapes=list(scratch),
        compiler_params=_params("arbitrary" if carries else "parallel"), name=name,
    )(h, *consts)


def _gla_kernel(x_ref, gk_ref, nw_ref, o_ref, state_ref, *, dk, dv):
    hk, hv = dk // GLA_HEADS, dv // GLA_HEADS
    scale = hk ** -0.5

    @pl.when(pl.program_id(1) == 0)
    def _():
        state_ref[...] = jnp.zeros_like(state_ref)

    tril = _tril(CHUNK)
    tril_b = tril.astype(BF16)
    nw = nw_ref[...]
    for j in range(x_ref.shape[0] // CHUNK):
        rows = slice(j * CHUNK, (j + 1) * CHUNK)
        b_all = _sel_left(tril_b, gk_ref[rows, :])
        for h in range(GLA_HEADS):
            q = x_ref[rows, h * hk:(h + 1) * hk].astype(F32) * scale
            k = x_ref[rows, dk + h * hk:dk + (h + 1) * hk].astype(F32)
            v = x_ref[rows, 2 * dk + h * hv:2 * dk + (h + 1) * hv]
            r = x_ref[rows, 2 * dk + dv + h * hv:2 * dk + dv + (h + 1) * hv].astype(F32)
            b = b_all[:, h * hk:(h + 1) * hk]
            b_last = b[CHUNK - 1:CHUNK, :]
            b_mid = b[CHUNK // 2:CHUNK // 2 + 1, :]
            a = _dot_nt((q * jnp.exp2(b - b_mid)).astype(BF16), (k * jnp.exp2(b_mid - b)).astype(BF16))
            a = jnp.where(tril, a, 0.0)
            st = state_ref[h]
            o = _dot(a.astype(BF16), v) + _dot_nt((q * jnp.exp2(b)).astype(BF16), st.astype(BF16))
            state_ref[h] = jnp.exp2(b_last) * st + _dot_tn(v, (k * jnp.exp2(b_last - b)).astype(BF16))
            o = _rms(o, nw) * (r * jax.nn.sigmoid(r))
            o_ref[rows, h * hv:(h + 1) * hv] = o.astype(BF16)


def _chunk_tril(n):
    pos = jnp.arange(n)
    return ((pos[:, None] >= pos[None, :]) & (pos[:, None] // CHUNK == pos[None, :] // CHUNK)).astype(BF16)


def _gla(x, gk, norm_w, dk, dv):
    bsz, seq, width = x.shape
    lb = min(GLA_SEQ_BLOCK, seq)
    hk, hv = dk // GLA_HEADS, dv // GLA_HEADS
    return pl.pallas_call(
        functools.partial(_gla_kernel, dk=dk, dv=dv),
        grid=(bsz, seq // lb),
        in_specs=[pl.BlockSpec((None, lb, width), lambda b, i: (b, i, 0)),
                  pl.BlockSpec((None, lb, dk), lambda b, i: (b, i, 0)),
                  _resident((1, hv))],
        out_specs=pl.BlockSpec((None, lb, dv), lambda b, i: (b, i, 0)),
        out_shape=jax.ShapeDtypeStruct((bsz, seq, dv), BF16),
        scratch_shapes=[pltpu.VMEM((GLA_HEADS, hv, hk), F32)],
        compiler_params=_params("parallel", "arbitrary"),
        name="gla",
    )(x, gk, norm_w)


def _ssd_kernel(x_ref, dt_ref, alog_ref, dexp_ref, nw_ref, tblk_ref, e3_ref, e2_ref, bd_ref,
                o_ref, state_ref, *, d_inner):
    lb = x_ref.shape[0]
    gn = SSM_GROUPS * SSM_DSTATE
    gw = d_inner // SSM_GROUPS
    hpg = gw // SSM_HEADDIM
    b_off, c_off = 2 * d_inner, 2 * d_inner + gn

    @pl.when(pl.program_id(1) == 0)
    def _():
        state_ref[...] = jnp.zeros_like(state_ref)

    dt = dt_ref[...]
    cum = _sel_left(tblk_ref[...], dt * (-LOG2_E * jnp.exp(alog_ref[...])))
    cum_exp = _dot(jnp.concatenate(_split3(cum), axis=1), e3_ref[...])
    dt_hi, dt_mid, _ = _split3(dt)
    dt_exp = _dot(jnp.concatenate([dt_hi, dt_mid], axis=1), e2_ref[...])

    row = lax.broadcasted_iota(jnp.int32, (CHUNK, d_inner), 0)
    key = lax.rem(lax.broadcasted_iota(jnp.int32, (CHUNK, d_inner), 1), SSM_HEADDIM)
    diag = row == key
    causal = (row >= key)[:, :gw]
    blockdiag = bd_ref[...]
    for j in range(lb // CHUNK):
        rows = slice(j * CHUNK, (j + 1) * CHUNK)
        ce = cum_exp[rows, :]
        cum_s = jnp.sum(jnp.where(diag, ce, 0.0), axis=0, keepdims=True)
        cum_last = ce[CHUNK - 1:CHUNK, :]
        for g in range(SSM_GROUPS):
            cols = slice(g * gw, (g + 1) * gw)
            bg = x_ref[rows, b_off + g * SSM_DSTATE:b_off + (g + 1) * SSM_DSTATE]
            cg = x_ref[rows, c_off + g * SSM_DSTATE:c_off + (g + 1) * SSM_DSTATE]
            cb = _dot_nt(cg, jnp.concatenate([bg, bg], axis=0))
            cb = jnp.concatenate([cb] * (hpg // 2), axis=1)
            ceg = ce[:, cols]
            lam = jnp.exp2(jnp.where(causal, ceg - cum_s[:, cols], -jnp.inf))
            m = (cb * lam).astype(BF16)
            xg = x_ref[rows, d_inner + g * gw:d_inner + (g + 1) * gw].astype(F32)
            xdt = xg * dt_exp[rows, cols]
            xbd = jnp.concatenate([xdt.astype(BF16)] * hpg, axis=0) * blockdiag
            st = state_ref[g]
            y = _dot(m, xbd) + _dot(cg, st.astype(BF16)) * jnp.exp2(ceg) + dexp_ref[:, cols] * xg
            to_end = jnp.exp2(cum_last[:, cols] - ceg)
            state_ref[g] = jnp.exp2(cum_last[:, cols]) * st + _dot_tn(bg, (xdt * to_end).astype(BF16))
            z = x_ref[rows, cols].astype(F32)
            o_ref[rows, cols] = _rms(y * (z * jax.nn.sigmoid(z)), nw_ref[:, cols]).astype(BF16)


def _ssd(x, dt, a_log, d_exp, norm_w, d_inner):
    bsz, seq, width = x.shape
    heads = dt.shape[-1]
    assert CHUNK == SSM_HEADDIM and d_inner == heads * SSM_HEADDIM
    lb = min(SEQ_BLOCK, seq)
    gw = d_inner // SSM_GROUPS
    tblk = _chunk_tril(lb)
    expand = (jnp.arange(heads)[:, None] == jnp.arange(d_inner)[None, :] // SSM_HEADDIM).astype(BF16)
    same_head = (jnp.arange(gw)[:, None] // CHUNK == jnp.arange(gw)[None, :] // SSM_HEADDIM).astype(BF16)
    consts = [a_log, d_exp, norm_w, tblk, jnp.tile(expand, (3, 1)), jnp.tile(expand, (2, 1)), same_head]
    return pl.pallas_call(
        functools.partial(_ssd_kernel, d_inner=d_inner),
        grid=(bsz, seq // lb),
        in_specs=[pl.BlockSpec((None, lb, width), lambda b, i: (b, i, 0)),
                  pl.BlockSpec((None, lb, heads), lambda b, i: (b, i, 0))] + [_resident(c.shape) for c in consts],
        out_specs=pl.BlockSpec((None, lb, d_inner), lambda b, i: (b, i, 0)),
        out_shape=jax.ShapeDtypeStruct((bsz, seq, d_inner), BF16),
        scratch_shapes=[pltpu.VMEM((SSM_GROUPS, SSM_DSTATE, gw), F32)],
        compiler_params=_params("parallel", "arbitrary"),
        name="ssd",
    )(x, dt, *consts)


ATT_Q_CHUNKS = 4
ATT_HEAD_GROUP = 4


def _att_kernel(x_ref, bias_ref, o_ref, *, width):
    seq = x_ref.shape[0]
    pad = ATT_LEFT_CHUNKS * CHUNK
    qrows = ATT_Q_CHUNKS * CHUNK
    win = pad + qrows
    gwid = ATT_HEAD_GROUP * ATT_HD
    scale = ATT_HD ** -0.5
    head_of_lane = lax.broadcasted_iota(jnp.int32, (qrows, gwid), 1) // ATT_HD

    def block(r0, k0, nk):
        n_heads = width // ATT_HD

        def scores(head):
            hq, h = divmod(head, ATT_HEAD_GROUP)
            q = x_ref[pl.ds(r0, qrows), hq * gwid:(hq + 1) * gwid] * scale
            kb = x_ref[pl.ds(k0, nk), width + hq * gwid:width + (hq + 1) * gwid]
            return _dot_nt(jnp.where(head_of_lane == h, q, 0).astype(BF16), kb)

        nxt = scores(0)
        o = None
        for head in range(n_heads):
            hq, h = divmod(head, ATT_HEAD_GROUP)
            s = nxt
            if head + 1 < n_heads:
                nxt = scores(head + 1)
            sh = s + bias_ref[head, :, win - nk:win]
            e = jnp.exp(sh - jnp.max(sh, axis=-1, keepdims=True))
            prob = (e * (1.0 / jnp.sum(e, axis=-1, keepdims=True))).astype(BF16)
            vb = x_ref[pl.ds(k0, nk), 2 * width + hq * gwid:2 * width + (hq + 1) * gwid]
            oh = _dot(prob, vb)
            o = oh if h == 0 else jnp.where(head_of_lane == h, oh, o)
            if h == ATT_HEAD_GROUP - 1:
                o_ref[pl.ds(r0, qrows), hq * gwid:(hq + 1) * gwid] = o.astype(BF16)

    n_head = pad // qrows
    for i in range(n_head):
        block(i * qrows, 0, (i + 1) * qrows)

    def body(i, carry):
        r0 = pl.multiple_of(i * qrows, qrows)
        block(r0, pl.multiple_of(r0 - pad, qrows), win)
        return carry

    lax.fori_loop(n_head, seq // qrows, body, 0)


def _att(x, bias, width):
    bsz, seq, _ = x.shape
    assert seq % (ATT_Q_CHUNKS * CHUNK) == 0 and (ATT_LEFT_CHUNKS * CHUNK) % (ATT_Q_CHUNKS * CHUNK) == 0
    assert width % (ATT_HEAD_GROUP * ATT_HD) == 0
    return pl.pallas_call(
        functools.partial(_att_kernel, width=width),
        grid=(bsz,),
        in_specs=[pl.BlockSpec((None, seq, 3 * width), lambda b: (b, 0, 0)), _resident(bias.shape)],
        out_specs=pl.BlockSpec((None, seq, width), lambda b: (b, 0, 0)),
        out_shape=jax.ShapeDtypeStruct((bsz, seq, width), BF16),
        compiler_params=_params("parallel"),
        name="att",
    )(x, bias)


def _att_bias(rel_table):
    pad = ATT_LEFT_CHUNKS * CHUNK
    qrows = ATT_Q_CHUNKS * CHUNK
    win = pad + qrows
    n = qrows + win
    m = jnp.arange(n)
    off = jnp.where(m < win, m, m - n)
    u = rel_table[jnp.clip(pad - off, -ATT_MAX_REL, ATT_MAX_REL) + ATT_MAX_REL].astype(F32).T
    toep = jnp.tile(u, (1, qrows))[:, :qrows * (n - 1)].reshape(-1, qrows, n - 1)[:, :, :win]
    qc = jnp.arange(qrows)[:, None] // CHUNK
    kc = jnp.arange(win)[None, :] // CHUNK
    return jnp.where((kc >= qc) & (kc <= qc + ATT_LEFT_CHUNKS), toep, -jnp.inf)


def _merge_kernel(h_ref, a_ref, s_ref, c_ref, g_ref, wa_ref, ws_ref, wc_ref, wo_ref, o_ref):
    d = h_ref.shape[1]
    ya = _dot(a_ref[...], wa_ref[...])
    ys = _dot(s_ref[...], ws_ref[...])
    yc = _dot(c_ref[...], wc_ref[...])
    merged = (g_ref[:, 0:d].astype(F32) * ya + g_ref[:, d:2 * d].astype(F32) * ys
              + g_ref[:, 2 * d:3 * d].astype(F32) * yc)
    o_ref[...] = h_ref[...] + _dot(merged.astype(BF16), wo_ref[...])


def _merge(h, o_gla, o_ssm, o_att, gates, wa, ws, wc, wo):
    t, d = h.shape
    bm = min(TOKEN_BLOCK, t)

    def row(cols):
        return pl.BlockSpec((bm, cols), lambda i: (i, 0))

    return pl.pallas_call(
        _merge_kernel, grid=(t // bm,),
        in_specs=[row(d), row(o_gla.shape[1]), row(o_ssm.shape[1]), row(o_att.shape[1]), row(3 * d),
                  _resident(wa.shape), _resident(ws.shape), _resident(wc.shape), _resident(wo.shape)],
        out_specs=row(d),
        out_shape=jax.ShapeDtypeStruct((t, d), F32),
        compiler_params=_params("parallel"),
        name="merge",
    )(h, o_gla, o_ssm, o_att, gates, wa, ws, wc, wo)


def _mixer_layer(h, bsz, seq, mix_norm, w_in, gla_w_gk, gla_b_gk, gla_norm, ssm_conv_w, ssm_conv_b, ssm_dt_bias,
                 ssm_A_log, ssm_D, ssm_norm, attn_rel_bias, gate_bias, w_branch_gla, w_branch_ssm, w_branch_attn,
                 w_out):
    t, d = h.shape
    rank, gla_dk = gla_w_gk.shape
    gla_dv = w_branch_gla.shape[0]
    d_inner = w_branch_ssm.shape[0]
    conv_dim = ssm_conv_w.shape[1]
    heads = ssm_A_log.shape[0]
    att_w = w_branch_attn.shape[0]
    sizes = (gla_dk, gla_dk, gla_dv, gla_dv, rank, d_inner, conv_dim, heads, att_w, att_w, att_w, 3 * d)
    offs = [0]
    for s in sizes:
        offs.append(offs[-1] + s)
    w = w_in.astype(BF16)
    nw = mix_norm.reshape(1, d)

    qkvr, gk, qkv, gates = _proj_call(
        _proj_main_kernel, "proj_main", h,
        [nw, w[:, offs[0]:offs[4]], w[:, offs[4]:offs[5]], gla_w_gk.astype(BF16), gla_b_gk.reshape(1, gla_dk),
         w[:, offs[8]:offs[11]], w[:, offs[11]:offs[12]], gate_bias.reshape(1, 3 * d)],
        [(offs[4], BF16), (gla_dk, F32), (3 * att_w, BF16), (3 * d, BF16)])
    zx, dt = _proj_call(
        functools.partial(_proj_ssm_kernel, d_inner=d_inner, blocks_per_seq=seq // min(SSM_TOKEN_BLOCK, seq)),
        "proj_ssm", h,
        [nw, w[:, offs[5]:offs[7]], w[:, offs[7]:offs[8]], ssm_dt_bias.reshape(1, heads),
         ssm_conv_w, ssm_conv_b.reshape(1, conv_dim)],
        [(offs[7] - offs[5], BF16), (heads, F32)],
        scratch=[pltpu.VMEM((CONV_HALO, conv_dim), F32),
                 pltpu.VMEM((2, CONV_HALO + min(SSM_TOKEN_BLOCK, seq), CONV_COL_TILE), F32),
                 pltpu.VMEM((min(SSM_TOKEN_BLOCK, seq), d), BF16)], carries=True, block=SSM_TOKEN_BLOCK)
    o_gla = _gla(qkvr.reshape(bsz, seq, -1), gk.reshape(bsz, seq, gla_dk),
                 gla_norm.reshape(1, -1), gla_dk, gla_dv)
    o_ssm = _ssd(zx.reshape(bsz, seq, -1), dt.reshape(bsz, seq, heads), ssm_A_log.reshape(1, heads),
                 jnp.repeat(ssm_D, SSM_HEADDIM).reshape(1, d_inner), ssm_norm.reshape(1, d_inner), d_inner)
    o_att = _att(qkv.reshape(bsz, seq, -1), _att_bias(attn_rel_bias), att_w)
    return _merge(h, o_gla.reshape(t, -1), o_ssm.reshape(t, -1), o_att.reshape(t, -1), gates,
                  w_branch_gla.astype(BF16), w_branch_ssm.astype(BF16), w_branch_attn.astype(BF16),
                  w_out.astype(BF16))


def kernel(x, ffn1_norm, ffn1_w_gate, ffn1_w_up, ffn1_w_down, mix_norm, w_in, gla_w_gk, gla_b_gk, gla_norm, ssm_conv_w, ssm_conv_b, ssm_dt_bias, ssm_A_log, ssm_D, ssm_norm, attn_rel_bias, gate_bias, w_branch_gla, w_branch_ssm, w_branch_attn, w_out, ffn2_norm, ffn2_w_gate, ffn2_w_up, ffn2_w_down, final_norm):
    bsz, seq, d = x.shape
    depth = w_in.shape[0]
    h = x.reshape(bsz * seq, d)
    fw = final_norm.reshape(1, d)
    for l in range(depth):
        h = _ffn(h, ffn1_norm[l].reshape(1, d), ffn1_w_gate[l].astype(BF16), ffn1_w_up[l].astype(BF16),
                 ffn1_w_down[l].astype(BF16), fw, False)
        h = _mixer_layer(h, bsz, seq, mix_norm[l], w_in[l], gla_w_gk[l], gla_b_gk[l], gla_norm[l], ssm_conv_w[l],
                         ssm_conv_b[l], ssm_dt_bias[l], ssm_A_log[l], ssm_D[l], ssm_norm[l], attn_rel_bias[l],
                         gate_bias[l], w_branch_gla[l], w_branch_ssm[l], w_branch_attn[l], w_out[l])
        h = _ffn(h, ffn2_norm[l].reshape(1, d), ffn2_w_gate[l].astype(BF16), ffn2_w_up[l].astype(BF16),
                 ffn2_w_down[l].astype(BF16), fw, l == depth - 1)
    return h.reshape(bsz, seq, d)
```

```python
import functools

import jax
import jax.numpy as jnp
from jax import lax
from jax.experimental import pallas as pl
from jax.experimental.pallas import tpu as pltpu

F32 = jnp.float32
BF16 = jnp.bfloat16

EPS = 1e-6
LOG2_E = 1.4426950408889634
CHUNK = 64
GLA_HEADS = 4
GLA_GATE_NORM = 16.0
SSM_HEADDIM = 64
SSM_GROUPS = 8
SSM_DSTATE = 128
SSM_CONV = 4
ATT_HD = 64
ATT_LEFT_CHUNKS = 8
ATT_MAX_REL = 256

V7X_VMEM_BYTES = 64 * 1024 * 1024
VMEM_LIMIT_BYTES = V7X_VMEM_BYTES - 8 * 1024 * 1024
LANES = 128
MXU_TILE = 256

TOKEN_BLOCK = 512
FFN_TOKEN_BLOCK = 1024
SSM_TOKEN_BLOCK = 1024
SEQ_BLOCK = 512
GLA_SEQ_BLOCK = 512


def _params(*semantics):
    return pltpu.CompilerParams(dimension_semantics=semantics, vmem_limit_bytes=VMEM_LIMIT_BYTES)


def _resident(shape):
    zeros = (0,) * len(shape)
    return pl.BlockSpec(shape, lambda *_: zeros, pipeline_mode=pl.Buffered(1))


def _dot(a, b):
    return jnp.dot(a, b, preferred_element_type=F32)


def _dot_nt(a, b):
    return lax.dot_general(a, b, (((1,), (1,)), ((), ())), preferred_element_type=F32)


def _dot_tn(a, b):
    return lax.dot_general(a, b, (((0,), (0,)), ((), ())), preferred_element_type=F32)


def _rms(x, w):
    return x * lax.rsqrt(jnp.mean(x * x, axis=-1, keepdims=True) + EPS) * w


def _split3(x):
    hi = x.astype(BF16)
    r = x - hi.astype(F32)
    mid = r.astype(BF16)
    lo = (r - mid.astype(F32)).astype(BF16)
    return hi, mid, lo


def _sel_left(sel, x):
    hi, mid, lo = _split3(x)
    return _dot(sel, hi) + _dot(sel, mid) + _dot(sel, lo)


def _tril(n):
    return lax.broadcasted_iota(jnp.int32, (n, n), 0) >= lax.broadcasted_iota(jnp.int32, (n, n), 1)


def _ffn_kernel(h_ref, nw_ref, wg_ref, wu_ref, wd_ref, fw_ref, o_ref, *, final_norm):
    x = h_ref[...]
    xn = _rms(x, nw_ref[...]).astype(BF16)
    fc = MXU_TILE
    f_chunks = wg_ref.shape[1] // fc

    def gate_up(i):
        return _dot(xn, wg_ref[:, i * fc:(i + 1) * fc]), _dot(xn, wu_ref[:, i * fc:(i + 1) * fc])

    acc = None
    nxt = gate_up(0)
    for i in range(f_chunks):
        g, u = nxt
        if i + 1 < f_chunks:
            nxt = gate_up(i + 1)
        a = (g * jax.nn.sigmoid(g) * u).astype(BF16)
        y = _dot(a, wd_ref[i * fc:(i + 1) * fc, :])
        acc = y if acc is None else acc + y
    out = x + 0.5 * acc
    if final_norm:
        out = _rms(out, fw_ref[...])
    o_ref[...] = out


def _ffn(h, norm_w, wg, wu, wd, final_w, final_norm):
    t, d = h.shape
    f_dim = wg.shape[1]
    assert f_dim % MXU_TILE == 0
    bm = min(FFN_TOKEN_BLOCK, t)
    row = pl.BlockSpec((bm, d), lambda i: (i, 0))
    return pl.pallas_call(
        functools.partial(_ffn_kernel, final_norm=final_norm),
        grid=(t // bm,),
        in_specs=[row, _resident((1, d)), _resident((d, f_dim)), _resident((d, f_dim)),
                  _resident((f_dim, d)), _resident((1, d))],
        out_specs=row,
        out_shape=jax.ShapeDtypeStruct((t, d), F32),
        compiler_params=_params("parallel"),
        name="ffn",
    )(h, norm_w, wg, wu, wd, final_w)


def _proj_main_kernel(h_ref, nw_ref, wgla_ref, wc_ref, wgk_ref, bgk_ref, watt_ref, wgate_ref, bgate_ref,
                      gla_ref, gk_ref, att_ref, gate_ref):
    xn = _rms(h_ref[...], nw_ref[...]).astype(BF16)
    gla_ref[...] = _dot(xn, wgla_ref[...]).astype(BF16)
    code = _dot(xn, wc_ref[...]).astype(BF16)
    pre = _dot(code, wgk_ref[...]) + bgk_ref[...]
    log_sig = jnp.minimum(pre, 0.0) - jnp.log(1.0 + jnp.exp(-jnp.abs(pre)))
    gk_ref[...] = log_sig * (LOG2_E / GLA_GATE_NORM)
    att_ref[...] = _dot(xn, watt_ref[...]).astype(BF16)
    gate_ref[...] = jax.nn.sigmoid(_dot(xn, wgate_ref[...]) + bgate_ref[...]).astype(BF16)


CONV_HALO = 8
CONV_COL_TILE = 512
CONV_ROW_TILE = 64


def _proj_ssm_kernel(h_ref, nw_ref, w_ref, wdt_ref, dtb_ref, cw_ref, cb_ref, o_ref, dt_ref, carry_ref, acc_ref,
                     xn_ref, *, d_inner, blocks_per_seq):
    assert SSM_CONV == 4
    bm = h_ref.shape[0]

    @pl.when(pl.program_id(0) % blocks_per_seq == 0)
    def _():
        carry_ref[...] = jnp.zeros_like(carry_ref)

    xn_ref[...] = _rms(h_ref[...], nw_ref[...]).astype(BF16)
    dt_ref[...] = jax.nn.softplus(_dot(xn_ref[...], wdt_ref[...]) + dtb_ref[...])
    n_conv = (w_ref.shape[1] - d_inner) // CONV_COL_TILE
    n_tiles = n_conv + d_inner // CONV_COL_TILE

    def matmul_tile(k):
        if k < n_conv:
            c0 = k * CONV_COL_TILE
            acc_ref[k % 2, 0:CONV_HALO, :] = carry_ref[:, c0:c0 + CONV_COL_TILE]
            acc_ref[k % 2, CONV_HALO:, :] = _dot(xn_ref[...], w_ref[:, d_inner + c0:d_inner + c0 + CONV_COL_TILE])
        else:
            c0 = (k - n_conv) * CONV_COL_TILE
            o_ref[:, c0:c0 + CONV_COL_TILE] = _dot(xn_ref[...], w_ref[:, c0:c0 + CONV_COL_TILE]).astype(BF16)

    def conv_tile(k):
        c0 = k * CONV_COL_TILE
        for r0 in range(0, bm, CONV_ROW_TILE):
            for l0 in range(0, CONV_COL_TILE, LANES):
                cols = slice(c0 + l0, c0 + l0 + LANES)
                x0 = acc_ref[k % 2, r0:r0 + CONV_HALO + CONV_ROW_TILE, l0:l0 + LANES]
                x1 = pltpu.roll(x0, 1, axis=0)
                near = cw_ref[3:4, cols] * x0 + cw_ref[2:3, cols] * x1
                far = pltpu.roll(cw_ref[1:2, cols] * x0 + cw_ref[0:1, cols] * x1, 2, axis=0)
                conv = (cb_ref[:, cols] + near + far)[CONV_HALO:, :]
                o_ref[r0:r0 + CONV_ROW_TILE, d_inner + c0 + l0:d_inner + c0 + l0 + LANES] = (
                    conv * jax.nn.sigmoid(conv)).astype(BF16)
        carry_ref[:, c0:c0 + CONV_COL_TILE] = acc_ref[k % 2, bm:bm + CONV_HALO, :]

    matmul_tile(0)
    for k in range(n_tiles):
        if k + 1 < n_tiles:
            matmul_tile(k + 1)
        if k < n_conv:
            conv_tile(k)


def _proj_call(kernel, name, h, consts, outs, scratch=(), carries=False, block=TOKEN_BLOCK):
    t, d = h.shape
    bm = min(block, t)
    in_specs = [pl.BlockSpec((bm, d), lambda i: (i, 0))] + [_resident(c.shape) for c in consts]
    out_specs = [pl.BlockSpec((bm, cols), lambda i: (i, 0)) for cols, _ in outs]
    out_shape = [jax.ShapeDtypeStruct((t, cols), dtype) for cols, dtype in outs]
    return pl.pallas_call(
        kernel, grid=(t // bm,), in_specs=in_specs, out_specs=out_specs, out_shape=out_shape,
        scratch_shapes=list(scratch),
        compiler_params=_params("arbitrary" if carries else "parallel"), name=name,
    )(h, *consts)


def _gla_kernel(x_ref, gk_ref, nw_ref, o_ref, state_ref, *, dk, dv):
    hk, hv = dk // GLA_HEADS, dv // GLA_HEADS
    scale = hk ** -0.5

    @pl.when(pl.program_id(1) == 0)
    def _():
        state_ref[...] = jnp.zeros_like(state_ref)

    tril = _tril(CHUNK)
    tril_b = tril.astype(BF16)
    nw = nw_ref[...]
    for j in range(x_ref.shape[0] // CHUNK):
        rows = slice(j * CHUNK, (j + 1) * CHUNK)
        b_all = _sel_left(tril_b, gk_ref[rows, :])
        for h in range(GLA_HEADS):
            q = x_ref[rows, h * hk:(h + 1) * hk].astype(F32) * scale
            k = x_ref[rows, dk + h * hk:dk + (h + 1) * hk].astype(F32)
            v = x_ref[rows, 2 * dk + h * hv:2 * dk + (h + 1) * hv]
            r = x_ref[rows, 2 * dk + dv + h * hv:2 * dk + dv + (h + 1) * hv].astype(F32)
            b = b_all[:, h * hk:(h + 1) * hk]
            b_last = b[CHUNK - 1:CHUNK, :]
            b_mid = b[CHUNK // 2:CHUNK // 2 + 1, :]
            a = _dot_nt((q * jnp.exp2(b - b_mid)).astype(BF16), (k * jnp.exp2(b_mid - b)).astype(BF16))
            a = jnp.where(tril, a, 0.0)
            st = state_ref[h]
            o = _dot(a.astype(BF16), v) + _dot_nt((q * jnp.exp2(b)).astype(BF16), st.astype(BF16))
            state_ref[h] = jnp.exp2(b_last) * st + _dot_tn(v, (k * jnp.exp2(b_last - b)).astype(BF16))
            o = _rms(o, nw) * (r * jax.nn.sigmoid(r))
            o_ref[rows, h * hv:(h + 1) * hv] = o.astype(BF16)


def _chunk_tril(n):
    pos = jnp.arange(n)
    return ((pos[:, None] >= pos[None, :]) & (pos[:, None] // CHUNK == pos[None, :] // CHUNK)).astype(BF16)


def _gla(x, gk, norm_w, dk, dv):
    bsz, seq, width = x.shape
    lb = min(GLA_SEQ_BLOCK, seq)
    hk, hv = dk // GLA_HEADS, dv // GLA_HEADS
    return pl.pallas_call(
        functools.partial(_gla_kernel, dk=dk, dv=dv),
        grid=(bsz, seq // lb),
        in_specs=[pl.BlockSpec((None, lb, width), lambda b, i: (b, i, 0)),
                  pl.BlockSpec((None, lb, dk), lambda b, i: (b, i, 0)),
                  _resident((1, hv))],
        out_specs=pl.BlockSpec((None, lb, dv), lambda b, i: (b, i, 0)),
        out_shape=jax.ShapeDtypeStruct((bsz, seq, dv), BF16),
        scratch_shapes=[pltpu.VMEM((GLA_HEADS, hv, hk), F32)],
        compiler_params=_params("parallel", "arbitrary"),
        name="gla",
    )(x, gk, norm_w)


def _ssd_kernel(x_ref, dt_ref, alog_ref, dexp_ref, nw_ref, tblk_ref, e3_ref, e2_ref, bd_ref,
                o_ref, state_ref, *, d_inner):
    lb = x_ref.shape[0]
    gn = SSM_GROUPS * SSM_DSTATE
    gw = d_inner // SSM_GROUPS
    hpg = gw // SSM_HEADDIM
    b_off, c_off = 2 * d_inner, 2 * d_inner + gn

    @pl.when(pl.program_id(1) == 0)
    def _():
        state_ref[...] = jnp.zeros_like(state_ref)

    dt = dt_ref[...]
    cum = _sel_left(tblk_ref[...], dt * (-LOG2_E * jnp.exp(alog_ref[...])))
    cum3 = jnp.concatenate(_split3(cum), axis=1)
    dt_hi, dt_mid, _ = _split3(dt)
    dt2 = jnp.concatenate([dt_hi, dt_mid], axis=1)

    row = lax.broadcasted_iota(jnp.int32, (CHUNK, d_inner), 0)
    key = lax.rem(lax.broadcasted_iota(jnp.int32, (CHUNK, d_inner), 1), SSM_HEADDIM)
    diag = row == key
    causal = (row >= key)[:, :gw]
    blockdiag = bd_ref[...]
    for j in range(lb // CHUNK):
        rows = slice(j * CHUNK, (j + 1) * CHUNK)
        ce = _dot(cum3[rows, :], e3_ref[...])
        dt_exp = _dot(dt2[rows, :], e2_ref[...])
        cum_s = jnp.sum(jnp.where(diag, ce, 0.0), axis=0, keepdims=True)
        cum_last = ce[CHUNK - 1:CHUNK, :]
        for g in range(SSM_GROUPS):
            cols = slice(g * gw, (g + 1) * gw)
            bg = x_ref[rows, b_off + g * SSM_DSTATE:b_off + (g + 1) * SSM_DSTATE]
            cg = x_ref[rows, c_off + g * SSM_DSTATE:c_off + (g + 1) * SSM_DSTATE]
            cb = _dot_nt(cg, jnp.concatenate([bg, bg], axis=0))
            cb = jnp.concatenate([cb] * (hpg // 2), axis=1)
            ceg = ce[:, cols]
            lam = jnp.exp2(jnp.where(causal, ceg - cum_s[:, cols], -jnp.inf))
            m = (cb * lam).astype(BF16)
            xg = x_ref[rows, d_inner + g * gw:d_inner + (g + 1) * gw].astype(F32)
            xdt = xg * dt_exp[:, cols]
            xbd = jnp.concatenate([xdt.astype(BF16)] * hpg, axis=0) * blockdiag
            st = state_ref[g]
            y = _dot(m, xbd) + _dot(cg, st.astype(BF16)) * jnp.exp2(ceg) + dexp_ref[:, cols] * xg
            to_end = jnp.exp2(cum_last[:, cols] - ceg)
            state_ref[g] = jnp.exp2(cum_last[:, cols]) * st + _dot_tn(bg, (xdt * to_end).astype(BF16))
            z = x_ref[rows, cols].astype(F32)
            o_ref[rows, cols] = _rms(y * (z * jax.nn.sigmoid(z)), nw_ref[:, cols]).astype(BF16)


def _ssd(x, dt, a_log, d_exp, norm_w, d_inner):
    bsz, seq, width = x.shape
    heads = dt.shape[-1]
    assert CHUNK == SSM_HEADDIM and d_inner == heads * SSM_HEADDIM
    lb = min(SEQ_BLOCK, seq)
    gw = d_inner // SSM_GROUPS
    tblk = _chunk_tril(lb)
    expand = (jnp.arange(heads)[:, None] == jnp.arange(d_inner)[None, :] // SSM_HEADDIM).astype(BF16)
    same_head = (jnp.arange(gw)[:, None] // CHUNK == jnp.arange(gw)[None, :] // SSM_HEADDIM).astype(BF16)
    consts = [a_log, d_exp, norm_w, tblk, jnp.tile(expand, (3, 1)), jnp.tile(expand, (2, 1)), same_head]
    return pl.pallas_call(
        functools.partial(_ssd_kernel, d_inner=d_inner),
        grid=(bsz, seq // lb),
        in_specs=[pl.BlockSpec((None, lb, width), lambda b, i: (b, i, 0)),
                  pl.BlockSpec((None, lb, heads), lambda b, i: (b, i, 0))] + [_resident(c.shape) for c in consts],
        out_specs=pl.BlockSpec((None, lb, d_inner), lambda b, i: (b, i, 0)),
        out_shape=jax.ShapeDtypeStruct((bsz, seq, d_inner), BF16),
        scratch_shapes=[pltpu.VMEM((SSM_GROUPS, SSM_DSTATE, gw), F32)],
        compiler_params=_params("parallel", "arbitrary"),
        name="ssd",
    )(x, dt, *consts)


ATT_Q_CHUNKS = 4
ATT_HEAD_GROUP = 4


def _att_kernel(x_ref, bias_ref, o_ref, *, width):
    seq = x_ref.shape[0]
    pad = ATT_LEFT_CHUNKS * CHUNK
    qrows = ATT_Q_CHUNKS * CHUNK
    win = pad + qrows
    gwid = ATT_HEAD_GROUP * ATT_HD
    scale = ATT_HD ** -0.5
    head_of_lane = lax.broadcasted_iota(jnp.int32, (qrows, gwid), 1) // ATT_HD

    def block(r0, k0, nk):
        n_heads = width // ATT_HD

        def scores(head):
            hq, h = divmod(head, ATT_HEAD_GROUP)
            q = x_ref[pl.ds(r0, qrows), hq * gwid:(hq + 1) * gwid] * scale
            kb = x_ref[pl.ds(k0, nk), width + hq * gwid:width + (hq + 1) * gwid]
            return _dot_nt(jnp.where(head_of_lane == h, q, 0).astype(BF16), kb)

        nxt = scores(0)
        o = None
        for head in range(n_heads):
            hq, h = divmod(head, ATT_HEAD_GROUP)
            s = nxt
            if head + 1 < n_heads:
                nxt = scores(head + 1)
            sh = s + bias_ref[head, :, win - nk:win]
            e = jnp.exp(sh - jnp.max(sh, axis=-1, keepdims=True))
            prob = (e * (1.0 / jnp.sum(e, axis=-1, keepdims=True))).astype(BF16)
            vb = x_ref[pl.ds(k0, nk), 2 * width + hq * gwid:2 * width + (hq + 1) * gwid]
            oh = _dot(prob, vb)
            o = oh if h == 0 else jnp.where(head_of_lane == h, oh, o)
            if h == ATT_HEAD_GROUP - 1:
                o_ref[pl.ds(r0, qrows), hq * gwid:(hq + 1) * gwid] = o.astype(BF16)

    n_head = pad // qrows
    for i in range(n_head):
        block(i * qrows, 0, (i + 1) * qrows)

    def body(i, carry):
        r0 = pl.multiple_of(i * qrows, qrows)
        block(r0, pl.multiple_of(r0 - pad, qrows), win)
        return carry

    lax.fori_loop(n_head, seq // qrows, body, 0)


def _att(x, bias, width):
    bsz, seq, _ = x.shape
    assert seq % (ATT_Q_CHUNKS * CHUNK) == 0 and (ATT_LEFT_CHUNKS * CHUNK) % (ATT_Q_CHUNKS * CHUNK) == 0
    assert width % (ATT_HEAD_GROUP * ATT_HD) == 0
    return pl.pallas_call(
        functools.partial(_att_kernel, width=width),
        grid=(bsz,),
        in_specs=[pl.BlockSpec((None, seq, 3 * width), lambda b: (b, 0, 0)), _resident(bias.shape)],
        out_specs=pl.BlockSpec((None, seq, width), lambda b: (b, 0, 0)),
        out_shape=jax.ShapeDtypeStruct((bsz, seq, width), BF16),
        compiler_params=_params("parallel"),
        name="att",
    )(x, bias)


def _att_bias(rel_table):
    pad = ATT_LEFT_CHUNKS * CHUNK
    qrows = ATT_Q_CHUNKS * CHUNK
    win = pad + qrows
    n = qrows + win
    m = jnp.arange(n)
    off = jnp.where(m < win, m, m - n)
    u = rel_table[jnp.clip(pad - off, -ATT_MAX_REL, ATT_MAX_REL) + ATT_MAX_REL].astype(F32).T
    toep = jnp.tile(u, (1, qrows))[:, :qrows * (n - 1)].reshape(-1, qrows, n - 1)[:, :, :win]
    qc = jnp.arange(qrows)[:, None] // CHUNK
    kc = jnp.arange(win)[None, :] // CHUNK
    return jnp.where((kc >= qc) & (kc <= qc + ATT_LEFT_CHUNKS), toep, -jnp.inf)


def _merge_kernel(h_ref, a_ref, s_ref, c_ref, g_ref, wa_ref, ws_ref, wc_ref, wo_ref, o_ref):
    d = h_ref.shape[1]
    ya = _dot(a_ref[...], wa_ref[...])
    ys = _dot(s_ref[...], ws_ref[...])
    yc = _dot(c_ref[...], wc_ref[...])
    merged = (g_ref[:, 0:d].astype(F32) * ya + g_ref[:, d:2 * d].astype(F32) * ys
              + g_ref[:, 2 * d:3 * d].astype(F32) * yc)
    o_ref[...] = h_ref[...] + _dot(merged.astype(BF16), wo_ref[...])


def _merge(h, o_gla, o_ssm, o_att, gates, wa, ws, wc, wo):
    t, d = h.shape
    bm = min(TOKEN_BLOCK, t)

    def row(cols):
        return pl.BlockSpec((bm, cols), lambda i: (i, 0))

    return pl.pallas_call(
        _merge_kernel, grid=(t // bm,),
        in_specs=[row(d), row(o_gla.shape[1]), row(o_ssm.shape[1]), row(o_att.shape[1]), row(3 * d),
                  _resident(wa.shape), _resident(ws.shape), _resident(wc.shape), _resident(wo.shape)],
        out_specs=row(d),
        out_shape=jax.ShapeDtypeStruct((t, d), F32),
        compiler_params=_params("parallel"),
        name="merge",
    )(h, o_gla, o_ssm, o_att, gates, wa, ws, wc, wo)


def _mixer_layer(h, bsz, seq, mix_norm, w_in, gla_w_gk, gla_b_gk, gla_norm, ssm_conv_w, ssm_conv_b, ssm_dt_bias,
                 ssm_A_log, ssm_D, ssm_norm, attn_rel_bias, gate_bias, w_branch_gla, w_branch_ssm, w_branch_attn,
                 w_out):
    t, d = h.shape
    rank, gla_dk = gla_w_gk.shape
    gla_dv = w_branch_gla.shape[0]
    d_inner = w_branch_ssm.shape[0]
    conv_dim = ssm_conv_w.shape[1]
    heads = ssm_A_log.shape[0]
    att_w = w_branch_attn.shape[0]
    sizes = (gla_dk, gla_dk, gla_dv, gla_dv, rank, d_inner, conv_dim, heads, att_w, att_w, att_w, 3 * d)
    offs = [0]
    for s in sizes:
        offs.append(offs[-1] + s)
    w = w_in.astype(BF16)
    nw = mix_norm.reshape(1, d)

    qkvr, gk, qkv, gates = _proj_call(
        _proj_main_kernel, "proj_main", h,
        [nw, w[:, offs[0]:offs[4]], w[:, offs[4]:offs[5]], gla_w_gk.astype(BF16), gla_b_gk.reshape(1, gla_dk),
         w[:, offs[8]:offs[11]], w[:, offs[11]:offs[12]], gate_bias.reshape(1, 3 * d)],
        [(offs[4], BF16), (gla_dk, F32), (3 * att_w, BF16), (3 * d, BF16)])
    zx, dt = _proj_call(
        functools.partial(_proj_ssm_kernel, d_inner=d_inner, blocks_per_seq=seq // min(SSM_TOKEN_BLOCK, seq)),
        "proj_ssm", h,
        [nw, w[:, offs[5]:offs[7]], w[:, offs[7]:offs[8]], ssm_dt_bias.reshape(1, heads),
         ssm_conv_w, ssm_conv_b.reshape(1, conv_dim)],
        [(offs[7] - offs[5], BF16), (heads, F32)],
        scratch=[pltpu.VMEM((CONV_HALO, conv_dim), F32),
                 pltpu.VMEM((2, CONV_HALO + min(SSM_TOKEN_BLOCK, seq), CONV_COL_TILE), F32),
                 pltpu.VMEM((min(SSM_TOKEN_BLOCK, seq), d), BF16)], carries=True, block=SSM_TOKEN_BLOCK)
    o_gla = _gla(qkvr.reshape(bsz, seq, -1), gk.reshape(bsz, seq, gla_dk),
                 gla_norm.reshape(1, -1), gla_dk, gla_dv)
    o_ssm = _ssd(zx.reshape(bsz, seq, -1), dt.reshape(bsz, seq, heads), ssm_A_log.reshape(1, heads),
                 jnp.repeat(ssm_D, SSM_HEADDIM).reshape(1, d_inner), ssm_norm.reshape(1, d_inner), d_inner)
    o_att = _att(qkv.reshape(bsz, seq, -1), _att_bias(attn_rel_bias), att_w)
    return _merge(h, o_gla.reshape(t, -1), o_ssm.reshape(t, -1), o_att.reshape(t, -1), gates,
                  w_branch_gla.astype(BF16), w_branch_ssm.astype(BF16), w_branch_attn.astype(BF16),
                  w_out.astype(BF16))


def kernel(x, ffn1_norm, ffn1_w_gate, ffn1_w_up, ffn1_w_down, mix_norm, w_in, gla_w_gk, gla_b_gk, gla_norm, ssm_conv_w, ssm_conv_b, ssm_dt_bias, ssm_A_log, ssm_D, ssm_norm, attn_rel_bias, gate_bias, w_branch_gla, w_branch_ssm, w_branch_attn, w_out, ffn2_norm, ffn2_w_gate, ffn2_w_up, ffn2_w_down, final_norm):
    bsz, seq, d = x.shape
    depth = w_in.shape[0]
    h = x.reshape(bsz * seq, d)
    fw = final_norm.reshape(1, d)
    for l in range(depth):
        h = _ffn(h, ffn1_norm[l].reshape(1, d), ffn1_w_gate[l].astype(BF16), ffn1_w_up[l].astype(BF16),
                 ffn1_w_down[l].astype(BF16), fw, False)
        h = _mixer_layer(h, bsz, seq, mix_norm[l], w_in[l], gla_w_gk[l], gla_b_gk[l], gla_norm[l], ssm_conv_w[l],
                         ssm_conv_b[l], ssm_dt_bias[l], ssm_A_log[l], ssm_D[l], ssm_norm[l], attn_rel_bias[l],
                         gate_bias[l], w_branch_gla[l], w_branch_ssm[l], w_branch_attn[l], w_out[l])
        h = _ffn(h, ffn2_norm[l].reshape(1, d), ffn2_w_gate[l].astype(BF16), ffn2_w_up[l].astype(BF16),
                 ffn2_w_down[l].astype(BF16), fw, l == depth - 1)
    return h.reshape(bsz, seq, d)
```
